```python
import jax, jax.numpy as jnp
from jax import lax
import numpy as np

D_MODEL = 1024
BATCH = 32
SEQ = 2048
DEPTH = 1

A_GROUPS = 8
A_GROUP_DIM = 64
A_WIDTH = A_GROUPS * A_GROUP_DIM
CHUNK = 128
N_HEADS = 8
N_KV_HEADS = 2
HEAD_DIM = 64
Q_GROUP = N_HEADS // N_KV_HEADS
B_WIDTH = N_HEADS * HEAD_DIM
KV_WIDTH = N_KV_HEADS * HEAD_DIM
WINDOW = 128
BLOCK = 128
SPAN = BLOCK + 2 * WINDOW
N_BUCKETS = 32
MAX_DISTANCE = 128
N_EXPERTS = 16
EXPERT_FF = 2048
CAPACITY_FACTOR = 2
EPS = 1e-6
IN_COLS = 2 * A_WIDTH + B_WIDTH + 2 * KV_WIDTH + 2 * D_MODEL
SPLITS = (A_WIDTH, 2 * A_WIDTH, 2 * A_WIDTH + B_WIDTH, 2 * A_WIDTH + B_WIDTH + KV_WIDTH,
          2 * A_WIDTH + B_WIDTH + 2 * KV_WIDTH, 2 * A_WIDTH + B_WIDTH + 2 * KV_WIDTH + D_MODEL)

kernel_name = "hybrid_gmlp_swa_ec_moe_encoder"


def rms_norm(x, g):
    xf = x.astype(jnp.float32)
    y = xf * lax.rsqrt(jnp.mean(xf * xf, axis=-1, keepdims=True) + EPS)
    return (y * g.astype(jnp.float32)).astype(x.dtype)


def t5_bucket(rel):
    nb = N_BUCKETS // 2
    max_exact = nb // 2
    ret = (rel > 0).astype(np.int32) * nb
    n = np.abs(rel)
    large = max_exact + (np.log(np.maximum(n, 1) / max_exact) / np.log(MAX_DISTANCE / max_exact)
                         * (nb - max_exact)).astype(np.int32)
    large = np.minimum(large, nb - 1)
    return (ret + np.where(n < max_exact, n, large)).astype(np.int32)


def chunked_spatial_gating(u, v, vnorm_g, w_spatial, b_spatial):
    B, S, _ = v.shape
    v = rms_norm(v, vnorm_g)
    vc = v.reshape(B, S // CHUNK, CHUNK, A_GROUPS, A_GROUP_DIM)
    mixed = jnp.einsum('gpq,bcqgd->bcpgd', w_spatial, vc) + b_spatial.T[None, None, :, :, None]
    return u * mixed.reshape(B, S, A_WIDTH)


def windowed_gqa(q, k, v, sink, rel_table):
    B, S = q.shape[0], q.shape[1]
    nb = S // BLOCK
    qb = q.reshape(B, nb, BLOCK, N_KV_HEADS, Q_GROUP, HEAD_DIM).transpose(1, 0, 3, 4, 2, 5)
    pad = ((0, 0), (WINDOW, WINDOW), (0, 0), (0, 0))
    kp = jnp.pad(k, pad).transpose(0, 2, 1, 3).astype(jnp.float32)
    vp = jnp.pad(v, pad).transpose(0, 2, 1, 3).astype(jnp.float32)
    rel = (np.arange(SPAN)[None, :] - WINDOW) - np.arange(BLOCK)[:, None]
    band = jnp.asarray(np.abs(rel) <= WINDOW)
    bias = rel_table[t5_bucket(rel)].astype(jnp.float32)
    bias = bias.transpose(2, 0, 1).reshape(N_KV_HEADS, Q_GROUP, BLOCK, SPAN)
    sink_f = sink.astype(jnp.float32).reshape(N_KV_HEADS, Q_GROUP, 1, 1)
    scale = HEAD_DIM ** -0.5

    def one_block(args):
        n, qblk = args
        start = n * BLOCK
        kblk = lax.dynamic_slice_in_dim(kp, start, SPAN, axis=2)
        vblk = lax.dynamic_slice_in_dim(vp, start, SPAN, axis=2)
        pos = start - WINDOW + jnp.arange(SPAN)
        valid = band & ((pos >= 0) & (pos < S))[None, :]
        s = jnp.einsum('bkgqd,bkjd->bkgqj', qblk.astype(jnp.float32), kblk) * scale + bias
        s = jnp.where(valid, s, -jnp.inf)
        m = jnp.maximum(jnp.max(s, axis=-1, keepdims=True), sink_f)
        p = jnp.exp(s - m)
        denom = jnp.sum(p, axis=-1, keepdims=True) + jnp.exp(sink_f - m)
        o = jnp.einsum('bkgqj,bkjd->bkgqd', p, vblk) / denom
        return o.astype(q.dtype)

    out = lax.map(one_block, (jnp.arange(nb), qb))
    return out.transpose(1, 0, 4, 2, 3, 5).reshape(B, S, B_WIDTH)


def expert_choice_moe(h, w_router, w_gate_e, w_up_e, w_down_e):
    B, S, _ = h.shape
    cap = CAPACITY_FACTOR * S // N_EXPERTS
    logits = jnp.einsum('bsd,de->bse', h.astype(jnp.float32), w_router.astype(jnp.float32))
    aff = jax.nn.softmax(logits, axis=-1)
    gates, idx = lax.top_k(aff.transpose(0, 2, 1), cap)
    bidx = jnp.arange(B)[:, None, None]
    xg = h[bidx, idx]
    hid = jax.nn.silu(jnp.einsum('becd,edf->becf', xg, w_gate_e)) * jnp.einsum('becd,edf->becf', xg, w_up_e)
    out = jnp.einsum('becf,efd->becd', hid, w_down_e) * gates[..., None].astype(h.dtype)
    return jnp.zeros_like(h).at[bidx, idx].add(out)


def setup_inputs(seed: int = 0) -> dict:
    key = jax.random.key(seed)
    ks = jax.random.split(key, 20)
    f32 = jnp.float32
    L = DEPTH
    nrm = lambda k, shape, s: jax.random.normal(k, shape, f32) * s
    return {
        "x": jax.random.normal(ks[0], (BATCH, SEQ, D_MODEL), f32),
        "norm_mix_g": 1.0 + nrm(ks[1], (L, D_MODEL), 0.02),
        "w_in": nrm(ks[2], (L, D_MODEL, IN_COLS), D_MODEL ** -0.5),
        "b_gate": nrm(ks[3], (L, 2 * D_MODEL), 0.02),
        "vnorm_g": 1.0 + nrm(ks[4], (L, A_WIDTH), 0.02),
        "w_spatial": nrm(ks[5], (L, A_GROUPS, CHUNK, CHUNK), CHUNK ** -0.5),
        "b_spatial": 1.0 + nrm(ks[6], (L, A_GROUPS, CHUNK), 0.02),
        "q_norm_g": 1.0 + nrm(ks[7], (L, HEAD_DIM), 0.02),
        "k_norm_g": 1.0 + nrm(ks[8], (L, HEAD_DIM), 0.02),
        "attn_sink": nrm(ks[9], (L, N_HEADS), 1.0),
        "rel_bias_table": nrm(ks[10], (N_BUCKETS, N_HEADS), 0.5),
        "w_proj_a": nrm(ks[11], (L, A_WIDTH, D_MODEL), A_WIDTH ** -0.5),
        "w_proj_b": nrm(ks[12], (L, B_WIDTH, D_MODEL), B_WIDTH ** -0.5),
        "w_out": nrm(ks[13], (L, D_MODEL, D_MODEL), D_MODEL ** -0.5),
        "norm_ffn_g": 1.0 + nrm(ks[14], (L, D_MODEL), 0.02),
        "w_router": nrm(ks[15], (L, D_MODEL, N_EXPERTS), D_MODEL ** -0.5),
        "w_gate_e": nrm(ks[16], (L, N_EXPERTS, D_MODEL, EXPERT_FF), D_MODEL ** -0.5),
        "w_up_e": nrm(ks[17], (L, N_EXPERTS, D_MODEL, EXPERT_FF), D_MODEL ** -0.5),
        "w_down_e": nrm(ks[18], (L, N_EXPERTS, EXPERT_FF, D_MODEL), EXPERT_FF ** -0.5),
    }


def reference(x, norm_mix_g, w_in, b_gate, vnorm_g, w_spatial, b_spatial, q_norm_g, k_norm_g,
              attn_sink, rel_bias_table, w_proj_a, w_proj_b, w_out, norm_ffn_g, w_router,
              w_gate_e, w_up_e, w_down_e):
    B, S, _ = x.shape
    for l in range(DEPTH):
        h = rms_norm(x, norm_mix_g[l])
        z = jnp.einsum('bsd,dc->bsc', h, w_in[l])
        u_a, v_a, q, k, v, g_a, g_b = jnp.split(z, SPLITS, axis=-1)
        a = chunked_spatial_gating(jax.nn.gelu(u_a), jax.nn.gelu(v_a), vnorm_g[l], w_spatial[l], b_spatial[l])
        q = rms_norm(q.reshape(B, S, N_HEADS, HEAD_DIM), q_norm_g[l])
        k = rms_norm(k.reshape(B, S, N_KV_HEADS, HEAD_DIM), k_norm_g[l])
        v = v.reshape(B, S, N_KV_HEADS, HEAD_DIM)
        o = windowed_gqa(q, k, v, attn_sink[l], rel_bias_table)
        gate_a = jax.nn.sigmoid(g_a + b_gate[l, :D_MODEL])
        gate_b = jax.nn.sigmoid(g_b + b_gate[l, D_MODEL:])
        merged = gate_a * jnp.einsum('bsa,ad->bsd', a, w_proj_a[l]) + gate_b * jnp.einsum('bsa,ad->bsd', o, w_proj_b[l])
        x = x + jnp.einsum('bsd,de->bse', merged, w_out[l])
        h = rms_norm(x, norm_ffn_g[l])
        x = x + expert_choice_moe(h, w_router[l], w_gate_e[l], w_up_e[l], w_down_e[l])
    return x
```

```python
import functools

import numpy as np
import jax
import jax.numpy as jnp
from jax import lax
from jax.experimental import pallas as pl
from jax.experimental.pallas import tpu as pltpu

F32 = jnp.float32
BF16 = jnp.bfloat16

D_MODEL = 1024
A_GROUPS = 8
A_GROUP_DIM = 64
A_WIDTH = A_GROUPS * A_GROUP_DIM
CHUNK = 128
N_HEADS = 8
N_KV_HEADS = 2
HEAD_DIM = 64
Q_GROUP = N_HEADS // N_KV_HEADS
B_WIDTH = N_HEADS * HEAD_DIM
KV_WIDTH = N_KV_HEADS * HEAD_DIM
WINDOW = 128
BLOCK = 128
SPAN = BLOCK + 2 * WINDOW
N_BUCKETS = 32
MAX_DISTANCE = 128
N_EXPERTS = 16
EXPERT_FF = 2048
CAPACITY_FACTOR = 2
EPS = 1e-6

LANES = 128
KV_DUP = 2 * KV_WIDTH
TM_INPROJ = 512
TQ_MIX = 512
FF_CHUNK = 512
MOE_GROUP = 2
TOPK_BATCHES = 8
TOPK_ROWS = 8
TOK_HI_LANE = 3 * N_EXPERTS
TOK_SHIFT = 6
VMEM_LIMIT = 56 * 1024 * 1024
INF_BITS = 0x7F800000


def _rms(x, g):
    return x * lax.rsqrt(jnp.mean(x * x, axis=-1, keepdims=True) + EPS) * g


def _inproj_kernel(x_ref, g_ref, w_ref, bg_ref, vg_ref, qg_ref, kg_ref, bdq_ref, bdk_ref,
                   u_ref, vn_ref, q_ref, k_ref, v_ref, ga_ref, gb_ref):
    hb = _rms(x_ref[...], g_ref[...]).astype(BF16)

    def seg(lo, hi):
        return jnp.dot(hb, w_ref[:, lo:hi], preferred_element_type=F32)

    c0 = 0
    u_ref[...] = jax.nn.gelu(seg(c0, c0 + A_WIDTH)).astype(BF16)
    c0 += A_WIDTH
    gv = jax.nn.gelu(seg(c0, c0 + A_WIDTH))
    vn_ref[...] = _rms(gv, vg_ref[...]).astype(BF16)
    c0 += A_WIDTH
    zq = seg(c0, c0 + B_WIDTH)
    msq = jnp.dot((zq * zq).astype(BF16), bdq_ref[...], preferred_element_type=F32)
    q_ref[...] = (zq * lax.rsqrt(msq + EPS) * qg_ref[...]).astype(BF16)
    c0 += B_WIDTH
    zk = seg(c0, c0 + KV_DUP)
    msk = jnp.dot((zk * zk).astype(BF16), bdk_ref[...], preferred_element_type=F32)
    k_ref[...] = (zk * lax.rsqrt(msk + EPS) * kg_ref[...]).astype(BF16)
    c0 += KV_DUP
    v_ref[...] = seg(c0, c0 + KV_DUP).astype(BF16)
    c0 += KV_DUP
    ga_ref[...] = jax.nn.sigmoid(seg(c0, c0 + D_MODEL) + bg_ref[:, :D_MODEL]).astype(BF16)
    c0 += D_MODEL
    gb_ref[...] = jax.nn.sigmoid(seg(c0, c0 + D_MODEL) + bg_ref[:, D_MODEL:]).astype(BF16)


def _inproj(xf, norm_g, w_all, b_gate, vnorm_g, qg, kg, bdq, bdk):
    T = xf.shape[0]
    tm = TM_INPROJ
    ncol = w_all.shape[1]
    row = lambda w: pl.BlockSpec((tm, w), lambda i: (i, 0))
    full = lambda a: pl.BlockSpec(a.shape, lambda i: (0,) * a.ndim)
    out_w = (A_WIDTH, A_WIDTH, B_WIDTH, KV_DUP, KV_DUP, D_MODEL, D_MODEL)
    return pl.pallas_call(
        _inproj_kernel,
        grid=(T // tm,),
        in_specs=[row(D_MODEL), full(norm_g), pl.BlockSpec((D_MODEL, ncol), lambda i: (0, 0)),
                  full(b_gate), full(vnorm_g), full(qg), full(kg), full(bdq), full(bdk)],
        out_specs=[row(w) for w in out_w],
        out_shape=[jax.ShapeDtypeStruct((T, w), BF16) for w in out_w],
        compiler_params=pltpu.CompilerParams(dimension_semantics=("arbitrary",),
                                             vmem_limit_bytes=VMEM_LIMIT),
        name="inproj",
    )(xf, norm_g, w_all, b_gate, vnorm_g, qg, kg, bdq, bdk)


def _block_diag_pair(slab, lo_mask):
    zero = jnp.zeros_like(slab)
    return jnp.concatenate([jnp.where(lo_mask, slab, zero), jnp.where(lo_mask, zero, slab)], axis=0)


def _mix_kernel(u_ref, vn_ref, q_ref, kc_ref, kp_ref, kn_ref, vc_ref, vp_ref, vx_ref,
                ga_ref, gb_ref, x_ref, wsp_ref, bsp_ref, bias_ref, sink_ref,
                wpa_ref, wpb_ref, wo_ref, ng_ref, wrh_ref, wrl_ref,
                x1_ref, aff_ref, val_ref, kbuf, vbuf, abuf, obuf):
    tq = x_ref.shape[0]
    nblk = tq // BLOCK
    i = pl.program_id(1)
    last = pl.num_programs(1) - 1
    lo_mask = lax.broadcasted_iota(jnp.int32, (1, LANES), 1) < HEAD_DIM

    kbuf[0:WINDOW, :] = kp_ref[...]
    kbuf[WINDOW:WINDOW + tq, :] = kc_ref[...]
    kbuf[WINDOW + tq:, :] = kn_ref[...]
    vbuf[0:WINDOW, :] = vp_ref[...]
    vbuf[WINDOW:WINDOW + tq, :] = vc_ref[...]
    vbuf[WINDOW + tq:, :] = vx_ref[...]

    for c in range(tq // CHUNK):
        rows = slice(c * CHUNK, (c + 1) * CHUNK)
        parts = []
        for j in range(A_GROUPS // 2):
            bd = _block_diag_pair(vn_ref[rows, j * LANES:(j + 1) * LANES], lo_mask)
            parts.append(jnp.dot(wsp_ref[j], bd, preferred_element_type=F32))
        mixed = jnp.concatenate(parts, axis=1) + bsp_ref[...]
        abuf[rows, :] = (u_ref[rows, :].astype(F32) * mixed).astype(BF16)

    for n in range(nblk):
        rows = slice(n * BLOCK, (n + 1) * BLOCK)
        if n == 0:
            var = jnp.where(i == 0, 0, 1)
        elif n == nblk - 1:
            var = jnp.where(i == last, 2, 1)
        else:
            var = 1
        span = slice(n * BLOCK, n * BLOCK + SPAN)
        for kh in range(N_KV_HEADS):
            ksl = slice(kh * LANES, (kh + 1) * LANES)
            kk = _block_diag_pair(kbuf[span, ksl], lo_mask)
            vv = _block_diag_pair(vbuf[span, ksl], lo_mask)
            q2 = jnp.concatenate([q_ref[rows, (2 * kh) * LANES:(2 * kh + 1) * LANES],
                                  q_ref[rows, (2 * kh + 1) * LANES:(2 * kh + 2) * LANES]], axis=0)
            s = lax.dot_general(q2, kk, (((1,), (1,)), ((), ())), preferred_element_type=F32)
            s = s + bias_ref[var, kh]
            ps, rs = [], []
            for half in range(2):
                sh = s[:, half * SPAN:(half + 1) * SPAN]
                sk = sink_ref[kh, half]
                m = jnp.maximum(jnp.max(sh, axis=-1, keepdims=True), sk)
                p = jnp.exp(sh - m)
                den = jnp.sum(p, axis=-1, keepdims=True) + jnp.exp(sk - m)
                ps.append(p.astype(BF16))
                rs.append(1.0 / den)
            o2 = jnp.dot(jnp.concatenate(ps, axis=1), vv, preferred_element_type=F32)
            o2 = o2 * jnp.where(lo_mask, rs[0], rs[1])
            obuf[rows, (2 * kh) * LANES:(2 * kh + 1) * LANES] = o2[:BLOCK].astype(BF16)
            obuf[rows, (2 * kh + 1) * LANES:(2 * kh + 2) * LANES] = o2[BLOCK:].astype(BF16)

    pa = jnp.dot(abuf[...], wpa_ref[...], preferred_element_type=F32)
    pb = jnp.dot(obuf[...], wpb_ref[...], preferred_element_type=F32)
    merged = ga_ref[...].astype(F32) * pa + gb_ref[...].astype(F32) * pb
    x1 = x_ref[...] + jnp.dot(merged.astype(BF16), wo_ref[...], preferred_element_type=F32)
    x1_ref[...] = x1

    h2 = _rms(x1, ng_ref[...])
    hi = h2.astype(BF16)
    lo = (h2 - hi.astype(F32)).astype(BF16)
    logits = (jnp.dot(hi, wrh_ref[...], preferred_element_type=F32)
              + jnp.dot(lo, wrh_ref[...], preferred_element_type=F32)
              + jnp.dot(hi, wrl_ref[...], preferred_element_type=F32))
    ex = jnp.exp(logits - jnp.max(logits, axis=-1, keepdims=True))
    aff = ex / jnp.sum(ex, axis=-1, keepdims=True)
    aff_ref[...] = aff

    lane_e = lax.broadcasted_iota(jnp.int32, (N_EXPERTS, LANES), 1)
    row_e = lax.broadcasted_iota(jnp.int32, (N_EXPERTS, LANES), 0)
    a_hi = aff.astype(BF16)
    r1 = aff - a_hi.astype(F32)
    a_mid = r1.astype(BF16)
    a_lo = (r1 - a_mid.astype(F32)).astype(BF16)
    vb = jnp.zeros((tq, LANES), F32)
    for k, piece in enumerate((a_hi, a_mid, a_lo)):
        place = jnp.where(lane_e == row_e + k * N_EXPERTS, 1.0, 0.0).astype(BF16)
        vb = vb + jnp.dot(piece, place, preferred_element_type=F32)
    lane = lax.broadcasted_iota(jnp.int32, (1, LANES), 1)
    tok = i * tq + lax.broadcasted_iota(jnp.int32, (tq, 1), 0)
    vb = jnp.where(lane == TOK_HI_LANE, (tok >> TOK_SHIFT).astype(F32),
                   jnp.where(lane == TOK_HI_LANE + 1, (tok & (2 ** TOK_SHIFT - 1)).astype(F32), vb))
    val_ref[...] = vb.astype(BF16)


def _mix(B, S, u, vn, q, k, v, ga, gb, xf, wsp, bsp, bias, sink, wpa, wpb, wo, ng, wrh, wrl):
    T = B * S
    tq = TQ_MIX
    nq = S // tq
    bpq = tq // BLOCK
    nb = S // BLOCK
    row = lambda w: pl.BlockSpec((tq, w), lambda b, i: (b * nq + i, 0))
    prev = pl.BlockSpec((BLOCK, KV_DUP), lambda b, i: (b * nb + jnp.maximum(i * bpq - 1, 0), 0))
    nxt = pl.BlockSpec((BLOCK, KV_DUP), lambda b, i: (b * nb + jnp.minimum(i * bpq + bpq, nb - 1), 0))
    full = lambda a: pl.BlockSpec(a.shape, lambda b, i: (0,) * a.ndim)
    return pl.pallas_call(
        _mix_kernel,
        grid=(B, nq),
        in_specs=[row(A_WIDTH), row(A_WIDTH), row(B_WIDTH),
                  row(KV_DUP), prev, nxt, row(KV_DUP), prev, nxt,
                  row(D_MODEL), row(D_MODEL), row(D_MODEL),
                  full(wsp), full(bsp), full(bias), full(sink),
                  full(wpa), full(wpb), full(wo), full(ng), full(wrh), full(wrl)],
        out_specs=[row(D_MODEL), row(N_EXPERTS), row(LANES)],
        out_shape=[jax.ShapeDtypeStruct((T, D_MODEL), F32),
                   jax.ShapeDtypeStruct((T, N_EXPERTS), F32),
                   jax.ShapeDtypeStruct((T, LANES), BF16)],
        scratch_shapes=[pltpu.VMEM((tq + 2 * WINDOW, KV_DUP), BF16),
                        pltpu.VMEM((tq + 2 * WINDOW, KV_DUP), BF16),
                        pltpu.VMEM((tq, A_WIDTH), BF16),
                        pltpu.VMEM((tq, B_WIDTH), BF16)],
        compiler_params=pltpu.CompilerParams(dimension_semantics=("arbitrary", "arbitrary"),
                                             vmem_limit_bytes=VMEM_LIMIT),
        name="mix",
    )(u, vn, q, k, k, k, v, v, v, ga, gb, xf, wsp, bsp, bias, sink, wpa, wpb, wo, ng, wrh, wrl)


def _topk_kernel(afft_ref, val_ref, tri_ref, idx_ref, gate_ref, posm, *, cap):
    R, S = afft_ref.shape
    j = pl.program_id(1)

    @pl.when(j == 0)
    def _():
        bits = pltpu.bitcast(afft_ref[...], jnp.int32)

        def search(_, carry):
            lo, hi = carry
            mid = lo + ((hi - lo) >> 1)
            cnt = jnp.sum(jnp.where(bits >= mid, 1.0, 0.0), axis=1, keepdims=True)
            ge = cnt >= cap
            return jnp.where(ge, mid, lo), jnp.where(ge, hi, mid)

        lo0 = jnp.zeros((R, 1), jnp.int32)
        hi0 = jnp.full((R, 1), INF_BITS, jnp.int32)
        thr, _ = lax.fori_loop(0, 31, search, (lo0, hi0))

        def prefix(mask):
            outs, carry = [], jnp.zeros((R, 1), F32)
            for jj in range(S // LANES):
                mj = mask[:, jj * LANES:(jj + 1) * LANES]
                outs.append(jnp.dot(mj.astype(BF16), tri_ref[...], preferred_element_type=F32) + carry)
                carry = carry + jnp.sum(mj, axis=1, keepdims=True)
            return jnp.concatenate(outs, axis=1)

        gt = jnp.where(bits > thr, 1.0, 0.0)
        eq = jnp.where(bits == thr, 1.0, 0.0)
        need = cap - jnp.sum(gt, axis=1, keepdims=True)
        sel = gt + eq * jnp.where(prefix(eq) < need, 1.0, 0.0)
        posm[...] = jnp.where(sel > 0.0, prefix(sel), -1.0)

    lane = lax.broadcasted_iota(jnp.int32, (1, LANES), 1)
    slot = lax.broadcasted_iota(jnp.int32, (cap, S), 0).astype(F32)

    def compact(rr, _):
        r = j * TOPK_ROWS + rr
        onehot = jnp.where(posm[pl.ds(r, 1), :] == slot, 1.0, 0.0).astype(BF16)
        out = jnp.dot(onehot, val_ref[r // N_EXPERTS], preferred_element_type=F32)
        e = r % N_EXPERTS
        gsel = (lane == e) | (lane == e + N_EXPERTS) | (lane == e + 2 * N_EXPERTS)
        gate_ref[rr] = jnp.sum(jnp.where(gsel, out, 0.0), axis=1, keepdims=True)
        tsel = jnp.where(lane == TOK_HI_LANE, out * float(2 ** TOK_SHIFT),
                         jnp.where(lane == TOK_HI_LANE + 1, out, 0.0))
        idx_ref[rr] = jnp.sum(tsel, axis=1, keepdims=True).astype(jnp.int32)
        return 0

    lax.fori_loop(0, TOPK_ROWS, compact, 0)


def _topk(B, S, cap, afft, val, tri):
    nb = min(TOPK_BATCHES, B)
    R = nb * N_EXPERTS
    steps = R // TOPK_ROWS
    out_block = pl.BlockSpec((TOPK_ROWS, cap, 1), lambda i, j: (i * steps + j, 0, 0))
    return pl.pallas_call(
        functools.partial(_topk_kernel, cap=cap),
        grid=(B // nb, steps),
        in_specs=[pl.BlockSpec((R, S), lambda i, j: (i, 0)),
                  pl.BlockSpec((nb, S, LANES), lambda i, j: (i, 0, 0)),
                  pl.BlockSpec(tri.shape, lambda i, j: (0, 0))],
        out_specs=[out_block, out_block],
        out_shape=[jax.ShapeDtypeStruct((B * N_EXPERTS, cap, 1), jnp.int32),
                   jax.ShapeDtypeStruct((B * N_EXPERTS, cap, 1), F32)],
        scratch_shapes=[pltpu.VMEM((R, S), F32)],
        compiler_params=pltpu.CompilerParams(dimension_semantics=("arbitrary", "arbitrary"),
                                             vmem_limit_bytes=VMEM_LIMIT),
        name="topk",
    )(afft, val, tri)


def _moe_kernel(idx_ref, x1_hbm, gate_ref, wg_ref, wu_ref, wd_ref, ng_ref, out_hbm,
                acc, xg, ybuf, sem_acc, sem_g, *, S, cap):
    G = MOE_GROUP
    g = pl.program_id(0)
    e = pl.program_id(1)
    nexp = pl.num_programs(1)
    rows0 = g * (G * S)

    def acc_copy(src, dst):
        return pltpu.make_async_copy(src, dst, sem_acc)

    def row_copy(src_row, dst_row):
        return pltpu.make_async_copy(x1_hbm.at[pl.ds(src_row, 1), :], xg.at[pl.ds(dst_row, 1), :], sem_g)

    @pl.when(e == 0)
    def _():
        cp = acc_copy(x1_hbm.at[pl.ds(rows0, G * S), :], acc)
        cp.start()
        cp.wait()

    def list_base(bl):
        return ((g * G + bl) * nexp + e) * cap

    for bl in range(G):
        def issue(c, _, bl=bl):
            row_copy(rows0 + bl * S + idx_ref[list_base(bl) + c], bl * cap + c).start()
            return 0
        lax.fori_loop(0, cap, issue, 0)

    def drain(c, _):
        row_copy(0, 0).wait()
        return 0
    lax.fori_loop(0, G * cap, drain, 0)

    for bl in range(G):
        hb = _rms(xg[bl * cap:(bl + 1) * cap, :], ng_ref[...]).astype(BF16)
        y = jnp.zeros((cap, D_MODEL), F32)
        for f in range(EXPERT_FF // FF_CHUNK):
            cols = slice(f * FF_CHUNK, (f + 1) * FF_CHUNK)
            gt = jnp.dot(hb, wg_ref[0, :, cols], preferred_element_type=F32)
            up = jnp.dot(hb, wu_ref[0, :, cols], preferred_element_type=F32)
            hid = (gt * jax.nn.sigmoid(gt) * up).astype(BF16)
            y = y + jnp.dot(hid, wd_ref[0, cols, :], preferred_element_type=F32)
        ybuf[...] = y * gate_ref[bl, 0]

        def scatter(c, _, bl=bl):
            r = bl * S + idx_ref[list_base(bl) + c]
            acc[pl.ds(r, 1), :] = acc[pl.ds(r, 1), :] + ybuf[pl.ds(c, 1), :]
            return 0
        lax.fori_loop(0, cap, scatter, 0)

    @pl.when(e == nexp - 1)
    def _():
        cp = acc_copy(acc, out_hbm.at[pl.ds(rows0, G * S), :])
        cp.start()
        cp.wait()


def _moe(B, S, cap, idx_flat, x1, gates, wg, wu, wd, ng):
    G = MOE_GROUP
    T = B * S
    grid_spec = pltpu.PrefetchScalarGridSpec(
        num_scalar_prefetch=1,
        grid=(B // G, N_EXPERTS),
        in_specs=[pl.BlockSpec(memory_space=pl.ANY),
                  pl.BlockSpec((G, 1, cap, 1), lambda g, e, idx: (g, e, 0, 0)),
                  pl.BlockSpec((1, D_MODEL, EXPERT_FF), lambda g, e, idx: (e, 0, 0)),
                  pl.BlockSpec((1, D_MODEL, EXPERT_FF), lambda g, e, idx: (e, 0, 0)),
                  pl.BlockSpec((1, EXPERT_FF, D_MODEL), lambda g, e, idx: (e, 0, 0)),
                  pl.BlockSpec((1, D_MODEL), lambda g, e, idx: (0, 0))],
        out_specs=pl.BlockSpec(memory_space=pl.ANY),
        scratch_shapes=[pltpu.VMEM((G * S, D_MODEL), F32),
                        pltpu.VMEM((G * cap, D_MODEL), F32),
                        pltpu.VMEM((cap, D_MODEL), F32),
                        pltpu.SemaphoreType.DMA(()),
                        pltpu.SemaphoreType.DMA(())],
    )
    return pl.pallas_call(
        functools.partial(_moe_kernel, S=S, cap=cap),
        grid_spec=grid_spec,
        out_shape=jax.ShapeDtypeStruct((T, D_MODEL), F32),
        input_output_aliases={1: 0},
        compiler_params=pltpu.CompilerParams(dimension_semantics=("arbitrary", "arbitrary"),
                                             vmem_limit_bytes=VMEM_LIMIT),
        name="moe",
    )(idx_flat, x1, gates, wg, wu, wd, ng)


def _t5_bucket(rel):
    nb = N_BUCKETS // 2
    max_exact = nb // 2
    ret = (rel > 0).astype(np.int32) * nb
    n = np.abs(rel)
    large = max_exact + (np.log(np.maximum(n, 1) / max_exact) / np.log(MAX_DISTANCE / max_exact)
                         * (nb - max_exact)).astype(np.int32)
    large = np.minimum(large, nb - 1)
    return (ret + np.where(n < max_exact, n, large)).astype(np.int32)


def _attention_bias(rel_table):
    rel = (np.arange(SPAN)[None, :] - WINDOW) - np.arange(BLOCK)[:, None]
    band = np.abs(rel) <= WINDOW
    col = np.arange(SPAN)[None, :]
    bias = rel_table[_t5_bucket(rel)].astype(F32).transpose(2, 0, 1)
    variants = []
    for valid in (band & (col >= WINDOW), band, band & (col < WINDOW + BLOCK)):
        bv = jnp.where(jnp.asarray(valid)[None], bias, -jnp.inf)
        bv = bv.reshape(N_KV_HEADS, 2, 2, BLOCK, SPAN)
        variants.append(bv.transpose(0, 1, 3, 2, 4).reshape(N_KV_HEADS, 2 * BLOCK, 2 * SPAN))
    return jnp.stack(variants)


def _dup_heads(w):
    parts = []
    for h in range(N_KV_HEADS):
        wh = w[..., h * HEAD_DIM:(h + 1) * HEAD_DIM]
        parts += [wh, wh]
    return jnp.concatenate(parts, axis=-1)


def _mean_matrix(width):
    blk = np.arange(width) // HEAD_DIM
    return jnp.asarray((blk[:, None] == blk[None, :]).astype(np.float32) / HEAD_DIM, dtype=BF16)


def kernel(x, norm_mix_g, w_in, b_gate, vnorm_g, w_spatial, b_spatial, q_norm_g, k_norm_g,
           attn_sink, rel_bias_table, w_proj_a, w_proj_b, w_out, norm_ffn_g, w_router,
           w_gate_e, w_up_e, w_down_e):
    B, S, _ = x.shape
    T = B * S
    cap = CAPACITY_FACTOR * S // N_EXPERTS
    assert S % TQ_MIX == 0 and T % TM_INPROJ == 0 and B % MOE_GROUP == 0
    assert B % min(TOPK_BATCHES, B) == 0
    xf = x.reshape(T, D_MODEL)
    l = 0

    w = w_in[l]
    o_q = 2 * A_WIDTH
    o_k = o_q + B_WIDTH
    o_v = o_k + KV_WIDTH
    o_g = o_v + KV_WIDTH
    w_all = jnp.concatenate([w[:, :o_k], _dup_heads(w[:, o_k:o_v]), _dup_heads(w[:, o_v:o_g]),
                             w[:, o_g:]], axis=1).astype(BF16)
    qg = jnp.tile(q_norm_g[l], N_HEADS)[None, :] * (HEAD_DIM ** -0.5)
    kg = jnp.tile(k_norm_g[l], 2 * N_KV_HEADS)[None, :]
    wsp = w_spatial[l].reshape(A_GROUPS // 2, 2, CHUNK, CHUNK).transpose(0, 2, 1, 3)
    wsp = wsp.reshape(A_GROUPS // 2, CHUNK, 2 * CHUNK).astype(BF16)
    bsp = jnp.repeat(b_spatial[l].T, A_GROUP_DIM, axis=1)
    bias = _attention_bias(rel_bias_table)
    sink = attn_sink[l].astype(F32).reshape(N_KV_HEADS, 2, 2)
    sink = jnp.broadcast_to(sink.transpose(0, 2, 1)[:, :, :, None, None],
                            (N_KV_HEADS, 2, 2, BLOCK, 1)).reshape(N_KV_HEADS, 2, 2 * BLOCK, 1)
    wr = w_router[l]
    wrh = wr.astype(BF16)
    wrl = (wr - wrh.astype(F32)).astype(BF16)
    tri = jnp.asarray(np.triu(np.ones((LANES, LANES), np.float32), k=1), dtype=BF16)

    u, vn, q, k, v, ga, gb = _inproj(xf, norm_mix_g[l][None, :], w_all, b_gate[l][None, :],
                                     vnorm_g[l][None, :], qg, kg,
                                     _mean_matrix(B_WIDTH), _mean_matrix(KV_DUP))
    x1, aff, val = _mix(B, S, u, vn, q, k, v, ga, gb, xf, wsp, bsp, bias, sink,
                        w_proj_a[l].astype(BF16), w_proj_b[l].astype(BF16), w_out[l].astype(BF16),
                        norm_ffn_g[l][None, :], wrh, wrl)
    afft = aff.reshape(B, S, N_EXPERTS).transpose(0, 2, 1).reshape(B * N_EXPERTS, S)
    idx, gates = _topk(B, S, cap, afft, val.reshape(B, S, LANES), tri)
    out = _moe(B, S, cap, idx.reshape(-1), x1, gates.reshape(B, N_EXPERTS, cap, 1),
               w_gate_e[l].astype(BF16), w_up_e[l].astype(BF16), w_down_e[l].astype(BF16),
               norm_ffn_g[l][None, :])
    return out.reshape(B, S, D_MODEL)
```

```python
import functools

import numpy as np
import jax
import jax.numpy as jnp
from jax import lax
from jax.experimental import pallas as pl
from jax.experimental.pallas import tpu as pltpu

F32 = jnp.float32
BF16 = jnp.bfloat16

D_MODEL = 1024
A_GROUPS = 8
A_GROUP_DIM = 64
A_WIDTH = A_GROUPS * A_GROUP_DIM
CHUNK = 128
N_HEADS = 8
N_KV_HEADS = 2
HEAD_DIM = 64
Q_GROUP = N_HEADS // N_KV_HEADS
B_WIDTH = N_HEADS * HEAD_DIM
KV_WIDTH = N_KV_HEADS * HEAD_DIM
WINDOW = 128
BLOCK = 128
SPAN = BLOCK + 2 * WINDOW
N_BUCKETS = 32
MAX_DISTANCE = 128
N_EXPERTS = 16
EXPERT_FF = 2048
CAPACITY_FACTOR = 2
EPS = 1e-6

LANES = 128
KV_DUP = 2 * KV_WIDTH
TM_INPROJ = 512
TQ_MIX = 512
FF_CHUNK = 512
MOE_GROUP = 2
TOPK_BATCHES = 8
TOPK_ROWS = 8
TOK_HI_LANE = 3 * N_EXPERTS
TOK_SHIFT = 6
VMEM_LIMIT = 56 * 1024 * 1024
INF_BITS = 0x7F800000


def _rms(x, g):
    return x * lax.rsqrt(jnp.mean(x * x, axis=-1, keepdims=True) + EPS) * g


def _inproj_kernel(x_ref, g_ref, w_ref, bg_ref, vg_ref, qg_ref, kg_ref, bdq_ref, bdk_ref,
                   u_ref, vn_ref, q_ref, k_ref, v_ref, ga_ref, gb_ref):
    hb = _rms(x_ref[...], g_ref[...]).astype(BF16)

    def seg(lo, hi):
        return jnp.dot(hb, w_ref[:, lo:hi], preferred_element_type=F32)

    c0 = 0
    u_ref[...] = jax.nn.gelu(seg(c0, c0 + A_WIDTH)).astype(BF16)
    c0 += A_WIDTH
    gv = jax.nn.gelu(seg(c0, c0 + A_WIDTH))
    vn_ref[...] = _rms(gv, vg_ref[...]).astype(BF16)
    c0 += A_WIDTH
    zq = seg(c0, c0 + B_WIDTH)
    msq = jnp.dot((zq * zq).astype(BF16), bdq_ref[...], preferred_element_type=F32)
    q_ref[...] = (zq * lax.rsqrt(msq + EPS) * qg_ref[...]).astype(BF16)
    c0 += B_WIDTH
    zk = seg(c0, c0 + KV_DUP)
    msk = jnp.dot((zk * zk).astype(BF16), bdk_ref[...], preferred_element_type=F32)
    k_ref[...] = (zk * lax.rsqrt(msk + EPS) * kg_ref[...]).astype(BF16)
    c0 += KV_DUP
    v_ref[...] = seg(c0, c0 + KV_DUP).astype(BF16)
    c0 += KV_DUP
    ga_ref[...] = jax.nn.sigmoid(seg(c0, c0 + D_MODEL) + bg_ref[:, :D_MODEL]).astype(BF16)
    c0 += D_MODEL
    gb_ref[...] = jax.nn.sigmoid(seg(c0, c0 + D_MODEL) + bg_ref[:, D_MODEL:]).astype(BF16)


def _inproj(xf, norm_g, w_all, b_gate, vnorm_g, qg, kg, bdq, bdk):
    T = xf.shape[0]
    tm = TM_INPROJ
    ncol = w_all.shape[1]
    row = lambda w: pl.BlockSpec((tm, w), lambda i: (i, 0))
    full = lambda a: pl.BlockSpec(a.shape, lambda i: (0,) * a.ndim)
    out_w = (A_WIDTH, A_WIDTH, B_WIDTH, KV_DUP, KV_DUP, D_MODEL, D_MODEL)
    return pl.pallas_call(
        _inproj_kernel,
        grid=(T // tm,),
        in_specs=[row(D_MODEL), full(norm_g), pl.BlockSpec((D_MODEL, ncol), lambda i: (0, 0)),
                  full(b_gate), full(vnorm_g), full(qg), full(kg), full(bdq), full(bdk)],
        out_specs=[row(w) for w in out_w],
        out_shape=[jax.ShapeDtypeStruct((T, w), BF16) for w in out_w],
        compiler_params=pltpu.CompilerParams(dimension_semantics=("arbitrary",),
                                             vmem_limit_bytes=VMEM_LIMIT),
        name="inproj",
    )(xf, norm_g, w_all, b_gate, vnorm_g, qg, kg, bdq, bdk)


def _block_diag_pair(slab, lo_mask):
    zero = jnp.zeros_like(slab)
    return jnp.concatenate([jnp.where(lo_mask, slab, zero), jnp.where(lo_mask, zero, slab)], axis=0)


def _mix_kernel(u_ref, vn_ref, q_ref, kc_ref, kp_ref, kn_ref, vc_ref, vp_ref, vx_ref,
                ga_ref, gb_ref, x_ref, wsp_ref, bsp_ref, bias_ref, sink_ref,
                wpa_ref, wpb_ref, wo_ref, ng_ref, wrh_ref, wrl_ref,
                x1_ref, aff_ref, val_ref, kbuf, vbuf, abuf, obuf):
    tq = x_ref.shape[0]
    nblk = tq // BLOCK
    i = pl.program_id(1)
    last = pl.num_programs(1) - 1
    lo_mask = lax.broadcasted_iota(jnp.int32, (1, LANES), 1) < HEAD_DIM

    kbuf[0:WINDOW, :] = kp_ref[...]
    kbuf[WINDOW:WINDOW + tq, :] = kc_ref[...]
    kbuf[WINDOW + tq:, :] = kn_ref[...]
    vbuf[0:WINDOW, :] = vp_ref[...]
    vbuf[WINDOW:WINDOW + tq, :] = vc_ref[...]
    vbuf[WINDOW + tq:, :] = vx_ref[...]

    for c in range(tq // CHUNK):
        rows = slice(c * CHUNK, (c + 1) * CHUNK)
        parts = []
        for j in range(A_GROUPS // 2):
            bd = _block_diag_pair(vn_ref[rows, j * LANES:(j + 1) * LANES], lo_mask)
            parts.append(jnp.dot(wsp_ref[j], bd, preferred_element_type=F32))
        mixed = jnp.concatenate(parts, axis=1) + bsp_ref[...]
        abuf[rows, :] = (u_ref[rows, :].astype(F32) * mixed).astype(BF16)

    for n in range(nblk):
        rows = slice(n * BLOCK, (n + 1) * BLOCK)
        if n == 0:
            var = jnp.where(i == 0, 0, 1)
        elif n == nblk - 1:
            var = jnp.where(i == last, 2, 1)
        else:
            var = 1
        span = slice(n * BLOCK, n * BLOCK + SPAN)
        for kh in range(N_KV_HEADS):
            ksl = slice(kh * LANES, (kh + 1) * LANES)
            kk = _block_diag_pair(kbuf[span, ksl], lo_mask)
            vv = _block_diag_pair(vbuf[span, ksl], lo_mask)
            q2 = jnp.concatenate([q_ref[rows, (2 * kh) * LANES:(2 * kh + 1) * LANES],
                                  q_ref[rows, (2 * kh + 1) * LANES:(2 * kh + 2) * LANES]], axis=0)
            s = lax.dot_general(q2, kk, (((1,), (1,)), ((), ())), preferred_element_type=F32)
            s = s + bias_ref[var, kh]
            ps, rs = [], []
            for half in range(2):
                sh = s[:, half * SPAN:(half + 1) * SPAN]
                sk = sink_ref[kh, half]
                m = jnp.maximum(jnp.max(sh, axis=-1, keepdims=True), sk)
                p = jnp.exp(sh - m)
                den = jnp.sum(p, axis=-1, keepdims=True) + jnp.exp(sk - m)
                ps.append(p.astype(BF16))
                rs.append(1.0 / den)
            o2 = jnp.dot(jnp.concatenate(ps, axis=1), vv, preferred_element_type=F32)
            o2 = o2 * jnp.where(lo_mask, rs[0], rs[1])
            obuf[rows, (2 * kh) * LANES:(2 * kh + 1) * LANES] = o2[:BLOCK].astype(BF16)
            obuf[rows, (2 * kh + 1) * LANES:(2 * kh + 2) * LANES] = o2[BLOCK:].astype(BF16)

    pa = jnp.dot(abuf[...], wpa_ref[...], preferred_element_type=F32)
    pb = jnp.dot(obuf[...], wpb_ref[...], preferred_element_type=F32)
    merged = ga_ref[...].astype(F32) * pa + gb_ref[...].astype(F32) * pb
    x1 = x_ref[...] + jnp.dot(merged.astype(BF16), wo_ref[...], preferred_element_type=F32)
    x1_ref[...] = x1

    h2 = _rms(x1, ng_ref[...])
    hi = h2.astype(BF16)
    lo = (h2 - hi.astype(F32)).astype(BF16)
    logits = (jnp.dot(hi, wrh_ref[...], preferred_element_type=F32)
              + jnp.dot(lo, wrh_ref[...], preferred_element_type=F32)
              + jnp.dot(hi, wrl_ref[...], preferred_element_type=F32))
    ex = jnp.exp(logits - jnp.max(logits, axis=-1, keepdims=True))
    aff = ex / jnp.sum(ex, axis=-1, keepdims=True)
    aff_ref[...] = aff

    lane_e = lax.broadcasted_iota(jnp.int32, (N_EXPERTS, LANES), 1)
    row_e = lax.broadcasted_iota(jnp.int32, (N_EXPERTS, LANES), 0)
    a_hi = aff.astype(BF16)
    r1 = aff - a_hi.astype(F32)
    a_mid = r1.astype(BF16)
    a_lo = (r1 - a_mid.astype(F32)).astype(BF16)
    vb = jnp.zeros((tq, LANES), F32)
    for k, piece in enumerate((a_hi, a_mid, a_lo)):
        place = jnp.where(lane_e == row_e + k * N_EXPERTS, 1.0, 0.0).astype(BF16)
        vb = vb + jnp.dot(piece, place, preferred_element_type=F32)
    lane = lax.broadcasted_iota(jnp.int32, (1, LANES), 1)
    tok = i * tq + lax.broadcasted_iota(jnp.int32, (tq, 1), 0)
    vb = jnp.where(lane == TOK_HI_LANE, (tok >> TOK_SHIFT).astype(F32),
                   jnp.where(lane == TOK_HI_LANE + 1, (tok & (2 ** TOK_SHIFT - 1)).astype(F32), vb))
    val_ref[...] = vb.astype(BF16)


def _mix(B, S, u, vn, q, k, v, ga, gb, xf, wsp, bsp, bias, sink, wpa, wpb, wo, ng, wrh, wrl):
    T = B * S
    tq = TQ_MIX
    nq = S // tq
    bpq = tq // BLOCK
    nb = S // BLOCK
    row = lambda w: pl.BlockSpec((tq, w), lambda b, i: (b * nq + i, 0))
    prev = pl.BlockSpec((BLOCK, KV_DUP), lambda b, i: (b * nb + jnp.maximum(i * bpq - 1, 0), 0))
    nxt = pl.BlockSpec((BLOCK, KV_DUP), lambda b, i: (b * nb + jnp.minimum(i * bpq + bpq, nb - 1), 0))
    full = lambda a: pl.BlockSpec(a.shape, lambda b, i: (0,) * a.ndim)
    return pl.pallas_call(
        _mix_kernel,
        grid=(B, nq),
        in_specs=[row(A_WIDTH), row(A_WIDTH), row(B_WIDTH),
                  row(KV_DUP), prev, nxt, row(KV_DUP), prev, nxt,
                  row(D_MODEL), row(D_MODEL), row(D_MODEL),
                  full(wsp), full(bsp), full(bias), full(sink),
                  full(wpa), full(wpb), full(wo), full(ng), full(wrh), full(wrl)],
        out_specs=[row(D_MODEL), row(N_EXPERTS), row(LANES)],
        out_shape=[jax.ShapeDtypeStruct((T, D_MODEL), F32),
                   jax.ShapeDtypeStruct((T, N_EXPERTS), F32),
                   jax.ShapeDtypeStruct((T, LANES), BF16)],
        scratch_shapes=[pltpu.VMEM((tq + 2 * WINDOW, KV_DUP), BF16),
                        pltpu.VMEM((tq + 2 * WINDOW, KV_DUP), BF16),
                        pltpu.VMEM((tq, A_WIDTH), BF16),
                        pltpu.VMEM((tq, B_WIDTH), BF16)],
        compiler_params=pltpu.CompilerParams(dimension_semantics=("arbitrary", "arbitrary"),
                                             vmem_limit_bytes=VMEM_LIMIT),
        name="mix",
    )(u, vn, q, k, k, k, v, v, v, ga, gb, xf, wsp, bsp, bias, sink, wpa, wpb, wo, ng, wrh, wrl)


def _topk_kernel(afft_ref, val_ref, tri_ref, idx_ref, gate_ref, posm, *, cap):
    R, S = afft_ref.shape
    j = pl.program_id(1)

    @pl.when(j == 0)
    def _():
        bits = pltpu.bitcast(afft_ref[...], jnp.int32)

        def search(_, carry):
            lo, hi = carry
            mid = lo + ((hi - lo) >> 1)
            cnt = jnp.sum(jnp.where(bits >= mid, 1.0, 0.0), axis=1, keepdims=True)
            ge = cnt >= cap
            return jnp.where(ge, mid, lo), jnp.where(ge, hi, mid)

        lo0 = jnp.zeros((R, 1), jnp.int32)
        hi0 = jnp.full((R, 1), INF_BITS, jnp.int32)
        thr, _ = lax.fori_loop(0, 31, search, (lo0, hi0))

        def prefix(mask):
            outs, carry = [], jnp.zeros((R, 1), F32)
            for jj in range(S // LANES):
                mj = mask[:, jj * LANES:(jj + 1) * LANES]
                outs.append(jnp.dot(mj.astype(BF16), tri_ref[...], preferred_element_type=F32) + carry)
                carry = carry + jnp.sum(mj, axis=1, keepdims=True)
            return jnp.concatenate(outs, axis=1)

        gt = jnp.where(bits > thr, 1.0, 0.0)
        eq = jnp.where(bits == thr, 1.0, 0.0)
        need = cap - jnp.sum(gt, axis=1, keepdims=True)
        sel = gt + eq * jnp.where(prefix(eq) < need, 1.0, 0.0)
        posm[...] = jnp.where(sel > 0.0, prefix(sel), -1.0)

    lane = lax.broadcasted_iota(jnp.int32, (1, LANES), 1)
    slot = lax.broadcasted_iota(jnp.int32, (cap, S), 0).astype(F32)

    def compact(rr, _):
        r = j * TOPK_ROWS + rr
        onehot = jnp.where(posm[pl.ds(r, 1), :] == slot, 1.0, 0.0).astype(BF16)
        out = jnp.dot(onehot, val_ref[r // N_EXPERTS], preferred_element_type=F32)
        e = r % N_EXPERTS
        gsel = (lane == e) | (lane == e + N_EXPERTS) | (lane == e + 2 * N_EXPERTS)
        gate_ref[rr] = jnp.sum(jnp.where(gsel, out, 0.0), axis=1, keepdims=True)
        tsel = jnp.where(lane == TOK_HI_LANE, out * float(2 ** TOK_SHIFT),
                         jnp.where(lane == TOK_HI_LANE + 1, out, 0.0))
        idx_ref[rr] = jnp.sum(tsel, axis=1, keepdims=True).astype(jnp.int32)
        return 0

    lax.fori_loop(0, TOPK_ROWS, compact, 0)


def _topk(B, S, cap, afft, val, tri):
    nb = min(TOPK_BATCHES, B)
    R = nb * N_EXPERTS
    steps = R // TOPK_ROWS
    out_block = pl.BlockSpec((TOPK_ROWS, cap, 1), lambda i, j: (i * steps + j, 0, 0))
    return pl.pallas_call(
        functools.partial(_topk_kernel, cap=cap),
        grid=(B // nb, steps),
        in_specs=[pl.BlockSpec((R, S), lambda i, j: (i, 0)),
                  pl.BlockSpec((nb, S, LANES), lambda i, j: (i, 0, 0)),
                  pl.BlockSpec(tri.shape, lambda i, j: (0, 0))],
        out_specs=[out_block, out_block],
        out_shape=[jax.ShapeDtypeStruct((B * N_EXPERTS, cap, 1), jnp.int32),
                   jax.ShapeDtypeStruct((B * N_EXPERTS, cap, 1), F32)],
        scratch_shapes=[pltpu.VMEM((R, S), F32)],
        compiler_params=pltpu.CompilerParams(dimension_semantics=("arbitrary", "arbitrary"),
                                             vmem_limit_bytes=VMEM_LIMIT),
        name="topk",
    )(afft, val, tri)


def _moe_kernel(idx_ref, x1_hbm, gate_ref, wg_ref, wu_ref, wd_ref, ng_ref, out_hbm,
                acc, hbuf, xg, ybuf, sem, *, S, cap):
    g = pl.program_id(0)
    e = pl.program_id(1)
    fh = pl.program_id(2)
    nexp = pl.num_programs(1)
    half = D_MODEL // 2
    rows0 = g * (MOE_GROUP * S)
    hi_mask = jnp.int32(-65536)

    def lists(seq, ee):
        return ((g * MOE_GROUP + seq) * nexp + ee) * cap

    def gather(seq, ee, par):
        base = lists(seq, ee)
        for c in range(cap):
            r = seq * S + idx_ref[base + c]
            xg[par, seq, c:c + 1, :] = hbuf[pl.ds(r, 1), :]

    def scatter(seq, ee):
        base = lists(seq, ee)
        for c in range(cap):
            r = seq * S + idx_ref[base + c]
            acc[pl.ds(r, 1), :] = acc[pl.ds(r, 1), :] + ybuf[seq, c:c + 1, :]

    def ffn(seq, par):
        p = xg[par, seq]
        lo = pltpu.bitcast(p << 16, F32).astype(BF16)
        hi = pltpu.bitcast(p & hi_mask, F32).astype(BF16)
        y = jnp.zeros((cap, D_MODEL), F32)
        for f in range(wg_ref.shape[2] // FF_CHUNK):
            cols = slice(f * FF_CHUNK, (f + 1) * FF_CHUNK)
            gt = (jnp.dot(lo, wg_ref[0, :half, cols], preferred_element_type=F32)
                  + jnp.dot(hi, wg_ref[0, half:, cols], preferred_element_type=F32))
            up = (jnp.dot(lo, wu_ref[0, :half, cols], preferred_element_type=F32)
                  + jnp.dot(hi, wu_ref[0, half:, cols], preferred_element_type=F32))
            hid = (gt * jax.nn.sigmoid(gt) * up).astype(BF16)
            y = y + jnp.dot(hid, wd_ref[0, cols, :], preferred_element_type=F32)
        return y * gate_ref[seq, 0]

    @pl.when((e == 0) & (fh == 0))
    def _():
        cp = pltpu.make_async_copy(x1_hbm.at[pl.ds(rows0, MOE_GROUP * S), :], acc, sem)
        cp.start()
        cp.wait()

        def pack(i, _):
            rows = pl.ds(pl.multiple_of(i * cap, cap), cap)
            hb = _rms(acc[rows, :], ng_ref[...]).astype(BF16).astype(F32)
            u = pltpu.bitcast(hb, jnp.int32)
            hbuf[rows, :] = lax.shift_right_logical(u[:, :half], 16) | (u[:, half:] & hi_mask)
            return 0
        lax.fori_loop(0, MOE_GROUP * S // cap, pack, 0)
        ybuf[1] = jnp.zeros((cap, D_MODEL), F32)
        gather(0, 0, 0)
        gather(1, 0, 0)

    par = e % 2
    e_next = jnp.minimum(e + 1, nexp - 1)
    e_prev = jnp.maximum(e - 1, 0)

    @pl.when(fh == 0)
    def _():
        y0 = ffn(0, par)
        scatter(1, e_prev)
        ybuf[0] = y0
        y1 = ffn(1, par)
        gather(0, e_next, 1 - par)
        ybuf[1] = y1

    @pl.when(fh == 1)
    def _():
        ybuf[0] = ybuf[0] + ffn(0, par)
        gather(1, e_next, 1 - par)
        ybuf[1] = ybuf[1] + ffn(1, par)
        scatter(0, e)

    @pl.when((e == nexp - 1) & (fh == 1))
    def _():
        scatter(1, e)
        cp = pltpu.make_async_copy(acc, out_hbm.at[pl.ds(rows0, MOE_GROUP * S), :], sem)
        cp.start()
        cp.wait()


def _moe(B, S, cap, idx_flat, x1, gates, wg, wu, wd, ng):
    G = MOE_GROUP
    T = B * S
    fhw = EXPERT_FF // 2
    grid_spec = pltpu.PrefetchScalarGridSpec(
        num_scalar_prefetch=1,
        grid=(B // G, N_EXPERTS, 2),
        in_specs=[pl.BlockSpec(memory_space=pl.ANY),
                  pl.BlockSpec((G, 1, cap, 1), lambda g, e, h, idx: (g, e, 0, 0)),
                  pl.BlockSpec((1, D_MODEL, fhw), lambda g, e, h, idx: (e, 0, h)),
                  pl.BlockSpec((1, D_MODEL, fhw), lambda g, e, h, idx: (e, 0, h)),
                  pl.BlockSpec((1, fhw, D_MODEL), lambda g, e, h, idx: (e, h, 0)),
                  pl.BlockSpec((1, D_MODEL), lambda g, e, h, idx: (0, 0))],
        out_specs=pl.BlockSpec(memory_space=pl.ANY),
        scratch_shapes=[pltpu.VMEM((G * S, D_MODEL), F32),
                        pltpu.VMEM((G * S, D_MODEL // 2), jnp.int32),
                        pltpu.VMEM((2, G, cap, D_MODEL // 2), jnp.int32),
                        pltpu.VMEM((G, cap, D_MODEL), F32),
                        pltpu.SemaphoreType.DMA(())],
    )
    return pl.pallas_call(
        functools.partial(_moe_kernel, S=S, cap=cap),
        grid_spec=grid_spec,
        out_shape=jax.ShapeDtypeStruct((T, D_MODEL), F32),
        input_output_aliases={1: 0},
        compiler_params=pltpu.CompilerParams(dimension_semantics=("arbitrary",) * 3,
                                             vmem_limit_bytes=VMEM_LIMIT),
        name="moe",
    )(idx_flat, x1, gates, wg, wu, wd, ng)


def _t5_bucket(rel):
    nb = N_BUCKETS // 2
    max_exact = nb // 2
    ret = (rel > 0).astype(np.int32) * nb
    n = np.abs(rel)
    large = max_exact + (np.log(np.maximum(n, 1) / max_exact) / np.log(MAX_DISTANCE / max_exact)
                         * (nb - max_exact)).astype(np.int32)
    large = np.minimum(large, nb - 1)
    return (ret + np.where(n < max_exact, n, large)).astype(np.int32)


def _attention_bias(rel_table):
    rel = (np.arange(SPAN)[None, :] - WINDOW) - np.arange(BLOCK)[:, None]
    band = np.abs(rel) <= WINDOW
    col = np.arange(SPAN)[None, :]
    bias = rel_table[_t5_bucket(rel)].astype(F32).transpose(2, 0, 1)
    variants = []
    for valid in (band & (col >= WINDOW), band, band & (col < WINDOW + BLOCK)):
        bv = jnp.where(jnp.asarray(valid)[None], bias, -jnp.inf)
        bv = bv.reshape(N_KV_HEADS, 2, 2, BLOCK, SPAN)
        variants.append(bv.transpose(0, 1, 3, 2, 4).reshape(N_KV_HEADS, 2 * BLOCK, 2 * SPAN))
    return jnp.stack(variants)


def _dup_heads(w):
    parts = []
    for h in range(N_KV_HEADS):
        wh = w[..., h * HEAD_DIM:(h + 1) * HEAD_DIM]
        parts += [wh, wh]
    return jnp.concatenate(parts, axis=-1)


def _mean_matrix(width):
    blk = np.arange(width) // HEAD_DIM
    return jnp.asarray((blk[:, None] == blk[None, :]).astype(np.float32) / HEAD_DIM, dtype=BF16)


def kernel(x, norm_mix_g, w_in, b_gate, vnorm_g, w_spatial, b_spatial, q_norm_g, k_norm_g,
           attn_sink, rel_bias_table, w_proj_a, w_proj_b, w_out, norm_ffn_g, w_router,
           w_gate_e, w_up_e, w_down_e):
    B, S, _ = x.shape
    T = B * S
    cap = CAPACITY_FACTOR * S // N_EXPERTS
    assert S % TQ_MIX == 0 and T % TM_INPROJ == 0 and B % MOE_GROUP == 0
    assert B % min(TOPK_BATCHES, B) == 0
    xf = x.reshape(T, D_MODEL)
    l = 0

    w = w_in[l]
    o_q = 2 * A_WIDTH
    o_k = o_q + B_WIDTH
    o_v = o_k + KV_WIDTH
    o_g = o_v + KV_WIDTH
    w_all = jnp.concatenate([w[:, :o_k], _dup_heads(w[:, o_k:o_v]), _dup_heads(w[:, o_v:o_g]),
                             w[:, o_g:]], axis=1).astype(BF16)
    qg = jnp.tile(q_norm_g[l], N_HEADS)[None, :] * (HEAD_DIM ** -0.5)
    kg = jnp.tile(k_norm_g[l], 2 * N_KV_HEADS)[None, :]
    wsp = w_spatial[l].reshape(A_GROUPS // 2, 2, CHUNK, CHUNK).transpose(0, 2, 1, 3)
    wsp = wsp.reshape(A_GROUPS // 2, CHUNK, 2 * CHUNK).astype(BF16)
    bsp = jnp.repeat(b_spatial[l].T, A_GROUP_DIM, axis=1)
    bias = _attention_bias(rel_bias_table)
    sink = attn_sink[l].astype(F32).reshape(N_KV_HEADS, 2, 2)
    sink = jnp.broadcast_to(sink.transpose(0, 2, 1)[:, :, :, None, None],
                            (N_KV_HEADS, 2, 2, BLOCK, 1)).reshape(N_KV_HEADS, 2, 2 * BLOCK, 1)
    wr = w_router[l]
    wrh = wr.astype(BF16)
    wrl = (wr - wrh.astype(F32)).astype(BF16)
    tri = jnp.asarray(np.triu(np.ones((LANES, LANES), np.float32), k=1), dtype=BF16)

    u, vn, q, k, v, ga, gb = _inproj(xf, norm_mix_g[l][None, :], w_all, b_gate[l][None, :],
                                     vnorm_g[l][None, :], qg, kg,
                                     _mean_matrix(B_WIDTH), _mean_matrix(KV_DUP))
    x1, aff, val = _mix(B, S, u, vn, q, k, v, ga, gb, xf, wsp, bsp, bias, sink,
                        w_proj_a[l].astype(BF16), w_proj_b[l].astype(BF16), w_out[l].astype(BF16),
                        norm_ffn_g[l][None, :], wrh, wrl)
    afft = aff.reshape(B, S, N_EXPERTS).transpose(0, 2, 1).reshape(B * N_EXPERTS, S)
    idx, gates = _topk(B, S, cap, afft, val.reshape(B, S, LANES), tri)
    out = _moe(B, S, cap, idx.reshape(-1), x1, gates.reshape(B, N_EXPERTS, cap, 1),
               w_gate_e[l].astype(BF16), w_up_e[l].astype(BF16), w_down_e[l].astype(BF16),
               norm_ffn_g[l][None, :])
    return out.reshape(B, S, D_MODEL)
```

```python
import functools

import numpy as np
import jax
import jax.numpy as jnp
from jax import lax
from jax.experimental import pallas as pl
from jax.experimental.pallas import tpu as pltpu

F32 = jnp.float32
BF16 = jnp.bfloat16

D_MODEL = 1024
A_GROUPS = 8
A_GROUP_DIM = 64
A_WIDTH = A_GROUPS * A_GROUP_DIM
CHUNK = 128
N_HEADS = 8
N_KV_HEADS = 2
HEAD_DIM = 64
Q_GROUP = N_HEADS // N_KV_HEADS
B_WIDTH = N_HEADS * HEAD_DIM
KV_WIDTH = N_KV_HEADS * HEAD_DIM
WINDOW = 128
BLOCK = 128
SPAN = BLOCK + 2 * WINDOW
N_BUCKETS = 32
MAX_DISTANCE = 128
N_EXPERTS = 16
EXPERT_FF = 2048
CAPACITY_FACTOR = 2
EPS = 1e-6

LANES = 128
KV_DUP = 2 * KV_WIDTH
TM_INPROJ = 512
TQ_MIX = 512
FF_CHUNK = 256
MOE_GROUP = 2
SCATTER_ROWS = 8
TOPK_BATCHES = 8
TOPK_ROWS = 8
TOK_HI_LANE = 3 * N_EXPERTS
TOK_SHIFT = 6
VMEM_LIMIT = 56 * 1024 * 1024
INF_BITS = 0x7F800000


def _rms(x, g):
    return x * lax.rsqrt(jnp.mean(x * x, axis=-1, keepdims=True) + EPS) * g


def _inproj_kernel(x_ref, g_ref, w_ref, bg_ref, vg_ref, qg_ref, kg_ref, bdq_ref, bdk_ref,
                   u_ref, vn_ref, q_ref, k_ref, v_ref, ga_ref, gb_ref):
    hb = _rms(x_ref[...], g_ref[...]).astype(BF16)

    def seg(lo, hi):
        return jnp.dot(hb, w_ref[:, lo:hi], preferred_element_type=F32)

    c0 = 0
    u_ref[...] = jax.nn.gelu(seg(c0, c0 + A_WIDTH)).astype(BF16)
    c0 += A_WIDTH
    gv = jax.nn.gelu(seg(c0, c0 + A_WIDTH))
    vn_ref[...] = _rms(gv, vg_ref[...]).astype(BF16)
    c0 += A_WIDTH
    zq = seg(c0, c0 + B_WIDTH)
    msq = jnp.dot((zq * zq).astype(BF16), bdq_ref[...], preferred_element_type=F32)
    q_ref[...] = (zq * lax.rsqrt(msq + EPS) * qg_ref[...]).astype(BF16)
    c0 += B_WIDTH
    zk = seg(c0, c0 + KV_DUP)
    msk = jnp.dot((zk * zk).astype(BF16), bdk_ref[...], preferred_element_type=F32)
    k_ref[...] = (zk * lax.rsqrt(msk + EPS) * kg_ref[...]).astype(BF16)
    c0 += KV_DUP
    v_ref[...] = seg(c0, c0 + KV_DUP).astype(BF16)
    c0 += KV_DUP
    ga_ref[...] = jax.nn.sigmoid(seg(c0, c0 + D_MODEL) + bg_ref[:, :D_MODEL]).astype(BF16)
    c0 += D_MODEL
    gb_ref[...] = jax.nn.sigmoid(seg(c0, c0 + D_MODEL) + bg_ref[:, D_MODEL:]).astype(BF16)


def _inproj(xf, norm_g, w_all, b_gate, vnorm_g, qg, kg, bdq, bdk):
    T = xf.shape[0]
    tm = TM_INPROJ
    ncol = w_all.shape[1]
    row = lambda w: pl.BlockSpec((tm, w), lambda i: (i, 0))
    full = lambda a: pl.BlockSpec(a.shape, lambda i: (0,) * a.ndim)
    out_w = (A_WIDTH, A_WIDTH, B_WIDTH, KV_DUP, KV_DUP, D_MODEL, D_MODEL)
    return pl.pallas_call(
        _inproj_kernel,
        grid=(T // tm,),
        in_specs=[row(D_MODEL), full(norm_g), pl.BlockSpec((D_MODEL, ncol), lambda i: (0, 0)),
                  full(b_gate), full(vnorm_g), full(qg), full(kg), full(bdq), full(bdk)],
        out_specs=[row(w) for w in out_w],
        out_shape=[jax.ShapeDtypeStruct((T, w), BF16) for w in out_w],
        compiler_params=pltpu.CompilerParams(dimension_semantics=("arbitrary",),
                                             vmem_limit_bytes=VMEM_LIMIT),
        name="inproj",
    )(xf, norm_g, w_all, b_gate, vnorm_g, qg, kg, bdq, bdk)


def _block_diag_pair(slab, lo_mask):
    zero = jnp.zeros_like(slab)
    return jnp.concatenate([jnp.where(lo_mask, slab, zero), jnp.where(lo_mask, zero, slab)], axis=0)


def _mix_kernel(u_ref, vn_ref, q_ref, kc_ref, kp_ref, kn_ref, vc_ref, vp_ref, vx_ref,
                ga_ref, gb_ref, x_ref, wsp_ref, bsp_ref, bias_ref, sink_ref,
                wpa_ref, wpb_ref, wo_ref, ng_ref, wr_ref,
                x1_ref, aff_ref, val_ref, kbuf, vbuf, abuf, obuf):
    tq = x_ref.shape[0]
    nblk = tq // BLOCK
    i = pl.program_id(1)
    last = pl.num_programs(1) - 1
    lo_mask = lax.broadcasted_iota(jnp.int32, (1, LANES), 1) < HEAD_DIM

    kbuf[0:WINDOW, :] = kp_ref[...]
    kbuf[WINDOW:WINDOW + tq, :] = kc_ref[...]
    kbuf[WINDOW + tq:, :] = kn_ref[...]
    vbuf[0:WINDOW, :] = vp_ref[...]
    vbuf[WINDOW:WINDOW + tq, :] = vc_ref[...]
    vbuf[WINDOW + tq:, :] = vx_ref[...]

    for c in range(tq // CHUNK):
        rows = slice(c * CHUNK, (c + 1) * CHUNK)
        parts = []
        for j in range(A_GROUPS // 2):
            bd = _block_diag_pair(vn_ref[rows, j * LANES:(j + 1) * LANES], lo_mask)
            parts.append(jnp.dot(wsp_ref[j], bd, preferred_element_type=F32))
        mixed = jnp.concatenate(parts, axis=1) + bsp_ref[...]
        abuf[rows, :] = (u_ref[rows, :].astype(F32) * mixed).astype(BF16)

    for n in range(nblk):
        rows = slice(n * BLOCK, (n + 1) * BLOCK)
        if n == 0:
            var = jnp.where(i == 0, 0, 1)
        elif n == nblk - 1:
            var = jnp.where(i == last, 2, 1)
        else:
            var = 1
        span = slice(n * BLOCK, n * BLOCK + SPAN)
        for kh in range(N_KV_HEADS):
            ksl = slice(kh * LANES, (kh + 1) * LANES)
            kk = _block_diag_pair(kbuf[span, ksl], lo_mask)
            vv = _block_diag_pair(vbuf[span, ksl], lo_mask)
            q2 = jnp.concatenate([q_ref[rows, (2 * kh) * LANES:(2 * kh + 1) * LANES],
                                  q_ref[rows, (2 * kh + 1) * LANES:(2 * kh + 2) * LANES]], axis=0)
            s = lax.dot_general(q2, kk, (((1,), (1,)), ((), ())), preferred_element_type=F32)
            s = s + bias_ref[var, kh]
            ps, rs = [], []
            for half in range(2):
                sh = s[:, half * SPAN:(half + 1) * SPAN]
                sk = sink_ref[kh, half]
                m = jnp.maximum(jnp.max(sh, axis=-1, keepdims=True), sk)
                p = jnp.exp(sh - m)
                den = jnp.sum(p, axis=-1, keepdims=True) + jnp.exp(sk - m)
                ps.append(p.astype(BF16))
                rs.append(1.0 / den)
            o2 = jnp.dot(jnp.concatenate(ps, axis=1), vv, preferred_element_type=F32)
            o2 = o2 * jnp.where(lo_mask, rs[0], rs[1])
            obuf[rows, (2 * kh) * LANES:(2 * kh + 1) * LANES] = o2[:BLOCK].astype(BF16)
            obuf[rows, (2 * kh + 1) * LANES:(2 * kh + 2) * LANES] = o2[BLOCK:].astype(BF16)

    pa = jnp.dot(abuf[...], wpa_ref[...], preferred_element_type=F32)
    pb = jnp.dot(obuf[...], wpb_ref[...], preferred_element_type=F32)
    merged = ga_ref[...].astype(F32) * pa + gb_ref[...].astype(F32) * pb
    x1 = x_ref[...] + jnp.dot(merged.astype(BF16), wo_ref[...], preferred_element_type=F32)
    x1_ref[...] = x1

    h2 = _rms(x1, ng_ref[...])
    hi = h2.astype(BF16)
    lo = (h2 - hi.astype(F32)).astype(BF16)
    both = jnp.dot(hi, wr_ref[...], preferred_element_type=F32)
    logits = (both[:, :LANES] + jnp.dot(lo, wr_ref[:, :LANES], preferred_element_type=F32)
              + both[:, LANES:])
    lane = lax.broadcasted_iota(jnp.int32, (1, LANES), 1)
    first = lane < N_EXPERTS
    m = jnp.max(jnp.where(first, logits, -jnp.inf), axis=-1, keepdims=True)
    ex = jnp.where(lane < TOK_HI_LANE, jnp.exp(logits - m), 0.0)
    aff = ex / jnp.sum(jnp.where(first, ex, 0.0), axis=-1, keepdims=True)
    aff_ref[...] = aff[:, :N_EXPERTS]

    a_hi = aff.astype(BF16).astype(F32)
    r1 = aff - a_hi
    a_mid = r1.astype(BF16).astype(F32)
    a_lo = r1 - a_mid
    vb = jnp.where(first, a_hi, jnp.where(lane < 2 * N_EXPERTS, a_mid, a_lo))
    tok = i * tq + lax.broadcasted_iota(jnp.int32, (tq, 1), 0)
    vb = jnp.where(lane == TOK_HI_LANE, (tok >> TOK_SHIFT).astype(F32),
                   jnp.where(lane == TOK_HI_LANE + 1, (tok & (2 ** TOK_SHIFT - 1)).astype(F32), vb))
    val_ref[...] = vb.astype(BF16)


def _mix(B, S, u, vn, q, k, v, ga, gb, xf, wsp, bsp, bias, sink, wpa, wpb, wo, ng, wr2):
    T = B * S
    tq = TQ_MIX
    nq = S // tq
    bpq = tq // BLOCK
    nb = S // BLOCK
    row = lambda w: pl.BlockSpec((tq, w), lambda b, i: (b * nq + i, 0))
    prev = pl.BlockSpec((BLOCK, KV_DUP), lambda b, i: (b * nb + jnp.maximum(i * bpq - 1, 0), 0))
    nxt = pl.BlockSpec((BLOCK, KV_DUP), lambda b, i: (b * nb + jnp.minimum(i * bpq + bpq, nb - 1), 0))
    full = lambda a: pl.BlockSpec(a.shape, lambda b, i: (0,) * a.ndim)
    return pl.pallas_call(
        _mix_kernel,
        grid=(B, nq),
        in_specs=[row(A_WIDTH), row(A_WIDTH), row(B_WIDTH),
                  row(KV_DUP), prev, nxt, row(KV_DUP), prev, nxt,
                  row(D_MODEL), row(D_MODEL), row(D_MODEL),
                  full(wsp), full(bsp), full(bias), full(sink),
                  full(wpa), full(wpb), full(wo), full(ng), full(wr2)],
        out_specs=[row(D_MODEL), row(N_EXPERTS), row(LANES)],
        out_shape=[jax.ShapeDtypeStruct((T, D_MODEL), F32),
                   jax.ShapeDtypeStruct((T, N_EXPERTS), F32),
                   jax.ShapeDtypeStruct((T, LANES), BF16)],
        scratch_shapes=[pltpu.VMEM((tq + 2 * WINDOW, KV_DUP), BF16),
                        pltpu.VMEM((tq + 2 * WINDOW, KV_DUP), BF16),
                        pltpu.VMEM((tq, A_WIDTH), BF16),
                        pltpu.VMEM((tq, B_WIDTH), BF16)],
        compiler_params=pltpu.CompilerParams(dimension_semantics=("arbitrary", "arbitrary"),
                                             vmem_limit_bytes=VMEM_LIMIT),
        name="mix",
    )(u, vn, q, k, k, k, v, v, v, ga, gb, xf, wsp, bsp, bias, sink, wpa, wpb, wo, ng, wr2)


def _topk_kernel(afft_ref, val_ref, tri_ref, idx_ref, gate_ref, posm, *, cap):
    R, S = afft_ref.shape
    j = pl.program_id(1)

    @pl.when(j == 0)
    def _():
        bits = pltpu.bitcast(afft_ref[...], jnp.int32)

        def search(_, carry):
            lo, hi = carry
            mid = lo + ((hi - lo) >> 1)
            cnt = jnp.sum(jnp.where(bits >= mid, 1.0, 0.0), axis=1, keepdims=True)
            ge = cnt >= cap
            return jnp.where(ge, mid, lo), jnp.where(ge, hi, mid)

        lo0 = jnp.zeros((R, 1), jnp.int32)
        hi0 = jnp.full((R, 1), INF_BITS, jnp.int32)
        thr, _ = lax.fori_loop(0, 31, search, (lo0, hi0))

        def prefix(mask):
            outs, carry = [], jnp.zeros((R, 1), F32)
            for jj in range(S // LANES):
                mj = mask[:, jj * LANES:(jj + 1) * LANES]
                outs.append(jnp.dot(mj.astype(BF16), tri_ref[...], preferred_element_type=F32) + carry)
                carry = carry + jnp.sum(mj, axis=1, keepdims=True)
            return jnp.concatenate(outs, axis=1)

        gt = jnp.where(bits > thr, 1.0, 0.0)
        eq = jnp.where(bits == thr, 1.0, 0.0)
        need = cap - jnp.sum(gt, axis=1, keepdims=True)
        sel = gt + eq * jnp.where(prefix(eq) < need, 1.0, 0.0)
        posm[...] = jnp.where(sel > 0.0, prefix(sel), -1.0)

    lane = lax.broadcasted_iota(jnp.int32, (1, LANES), 1)
    slot = lax.broadcasted_iota(jnp.int32, (cap, S), 0).astype(F32)

    def compact(rr, _):
        r = j * TOPK_ROWS + rr
        onehot = jnp.where(posm[pl.ds(r, 1), :] == slot, 1.0, 0.0).astype(BF16)
        out = jnp.dot(onehot, val_ref[r // N_EXPERTS], preferred_element_type=F32)
        e = r % N_EXPERTS
        gsel = (lane == e) | (lane == e + N_EXPERTS) | (lane == e + 2 * N_EXPERTS)
        gate_ref[rr] = jnp.sum(jnp.where(gsel, out, 0.0), axis=1, keepdims=True)
        tsel = jnp.where(lane == TOK_HI_LANE, out * float(2 ** TOK_SHIFT),
                         jnp.where(lane == TOK_HI_LANE + 1, out, 0.0))
        idx_ref[rr] = jnp.sum(tsel, axis=1, keepdims=True).astype(jnp.int32)
        return 0

    lax.fori_loop(0, TOPK_ROWS, compact, 0)


def _topk(B, S, cap, afft, val, tri):
    nb = min(TOPK_BATCHES, B)
    R = nb * N_EXPERTS
    steps = R // TOPK_ROWS
    out_block = pl.BlockSpec((TOPK_ROWS, cap, 1), lambda i, j: (i * steps + j, 0, 0))
    return pl.pallas_call(
        functools.partial(_topk_kernel, cap=cap),
        grid=(B // nb, steps),
        in_specs=[pl.BlockSpec((R, S), lambda i, j: (i, 0)),
                  pl.BlockSpec((nb, S, LANES), lambda i, j: (i, 0, 0)),
                  pl.BlockSpec(tri.shape, lambda i, j: (0, 0))],
        out_specs=[out_block, out_block],
        out_shape=[jax.ShapeDtypeStruct((B * N_EXPERTS, cap, 1), jnp.int32),
                   jax.ShapeDtypeStruct((B * N_EXPERTS, cap, 1), F32)],
        scratch_shapes=[pltpu.VMEM((R, S), F32)],
        compiler_params=pltpu.CompilerParams(dimension_semantics=("arbitrary", "arbitrary"),
                                             vmem_limit_bytes=VMEM_LIMIT),
        name="topk",
    )(afft, val, tri)


def _moe_kernel(idx_ref, x1_hbm, gate_ref, wg_ref, wu_ref, wd_ref, ng_ref, out_hbm,
                acc, hbuf, xg, ybuf, sem_in, sem, *, S, cap):
    g = pl.program_id(0)
    e2 = pl.program_id(1)
    k = pl.program_id(2)
    nexp = 2 * pl.num_programs(1)
    half = D_MODEL // 2
    rows0 = g * (MOE_GROUP * S)
    hi_mask = jnp.int32(-65536)

    def lists(seq, ee):
        return ((g * MOE_GROUP + seq) * nexp + ee) * cap

    def gather_jobs(seq, ee, par):
        base = lists(seq, ee)

        def job(c0):
            for c in range(c0, c0 + SCATTER_ROWS):
                r = seq * S + idx_ref[base + c]
                xg[par, seq * cap + c:seq * cap + c + 1, :] = hbuf[pl.ds(r, 1), :]
        return [functools.partial(job, c0) for c0 in range(0, cap, SCATTER_ROWS)]

    def scatter_jobs(seq, ee, par):
        base = lists(seq, ee)

        def job(c0):
            rows = [seq * S + idx_ref[base + c] for c in range(c0, c0 + SCATTER_ROWS)]
            old = [acc[pl.ds(r, 1), :] for r in rows]
            for j, r in enumerate(rows):
                c = seq * cap + c0 + j
                acc[pl.ds(r, 1), :] = old[j] + ybuf[par, c:c + 1, :]
        return [functools.partial(job, c0) for c0 in range(0, cap, SCATTER_ROWS)]

    def ffn(par, accumulate, jobs):
        pieces = 2 * (wg_ref.shape[2] // FF_CHUNK) if jobs else 1
        share = -(-len(jobs) // pieces)
        jobs = list(jobs)

        def run_share():
            for job in jobs[:share]:
                job()
            del jobs[:share]

        p = xg[par]
        lo = pltpu.bitcast(p << 16, F32).astype(BF16)
        hi = pltpu.bitcast(p & hi_mask, F32).astype(BF16)
        x = jnp.concatenate([lo, hi], axis=1)
        hid = []
        for f in range(wg_ref.shape[2] // FF_CHUNK):
            cols = slice(f * FF_CHUNK, (f + 1) * FF_CHUNK)
            gt = jnp.dot(x, wg_ref[0, :, cols], preferred_element_type=F32)
            up = jnp.dot(x, wu_ref[0, :, cols], preferred_element_type=F32)
            hid.append((gt * jax.nn.sigmoid(gt) * up).astype(BF16))
            run_share()
        hid = jnp.concatenate(hid, axis=1)
        gate = jnp.concatenate([gate_ref[s, 0] for s in range(MOE_GROUP)], axis=0)
        for n in range(D_MODEL // FF_CHUNK):
            cols = slice(n * FF_CHUNK, (n + 1) * FF_CHUNK)
            y = jnp.dot(hid, wd_ref[0, :, cols], preferred_element_type=F32) * gate
            ybuf[par, :, cols] = ybuf[par, :, cols] + y if accumulate else y
            run_share()
        assert not jobs

    @pl.when((e2 == 0) & (k == 0))
    def _():
        def load(i):
            start = i * cap if isinstance(i, int) else pl.multiple_of(i * cap, cap)
            return pltpu.make_async_copy(x1_hbm.at[pl.ds(rows0 + start, cap), :],
                                         acc.at[pl.ds(start, cap), :], sem_in.at[i])

        for i in range(sem_in.shape[0]):
            load(i).start()

        def pack(i, _):
            load(i).wait()
            rows = pl.ds(pl.multiple_of(i * cap, cap), cap)
            hb = _rms(acc[rows, :], ng_ref[...]).astype(BF16).astype(F32)
            u = pltpu.bitcast(hb, jnp.int32)
            hbuf[rows, :] = lax.shift_right_logical(u[:, :half], 16) | (u[:, half:] & hi_mask)
            return 0
        lax.fori_loop(0, MOE_GROUP * S // cap, pack, 0)
        ybuf[1] = jnp.zeros(ybuf.shape[1:], F32)
        for seq in range(MOE_GROUP):
            for job in gather_jobs(seq, 0, 0):
                job()

    for kk in range(4):
        @pl.when(k == kk)
        def _(par=kk // 2, fh=kk % 2):
            e = 2 * e2 + par
            sc = scatter_jobs(fh, jnp.maximum(e - 1, 0), 1 - par)
            ga = gather_jobs(fh, jnp.minimum(e + 1, nexp - 1), 1 - par)
            ffn(par, fh == 1, [job for pair in zip(sc, ga) for job in pair])

    @pl.when((e2 == pl.num_programs(1) - 1) & (k == 3))
    def _():
        for seq in range(MOE_GROUP):
            for job in scatter_jobs(seq, nexp - 1, 1):
                job()
        cp = pltpu.make_async_copy(acc, out_hbm.at[pl.ds(rows0, MOE_GROUP * S), :], sem)
        cp.start()
        cp.wait()


def _moe(B, S, cap, idx_flat, x1, gates, wg, wu, wd, ng):
    G = MOE_GROUP
    T = B * S
    fhw = EXPERT_FF // 2
    ex = lambda e2, k: 2 * e2 + k // 2
    fh = lambda k: k % 2
    grid_spec = pltpu.PrefetchScalarGridSpec(
        num_scalar_prefetch=1,
        grid=(B // G, N_EXPERTS // 2, 4),
        in_specs=[pl.BlockSpec(memory_space=pl.ANY),
                  pl.BlockSpec((G, 1, cap, 1), lambda g, e2, k, idx: (g, ex(e2, k), 0, 0)),
                  pl.BlockSpec((1, D_MODEL, fhw), lambda g, e2, k, idx: (ex(e2, k), 0, fh(k))),
                  pl.BlockSpec((1, D_MODEL, fhw), lambda g, e2, k, idx: (ex(e2, k), 0, fh(k))),
                  pl.BlockSpec((1, fhw, D_MODEL), lambda g, e2, k, idx: (ex(e2, k), fh(k), 0)),
                  pl.BlockSpec((1, D_MODEL), lambda g, e2, k, idx: (0, 0))],
        out_specs=pl.BlockSpec(memory_space=pl.ANY),
        scratch_shapes=[pltpu.VMEM((G * S, D_MODEL), F32),
                        pltpu.VMEM((G * S, D_MODEL // 2), jnp.int32),
                        pltpu.VMEM((2, G * cap, D_MODEL // 2), jnp.int32),
                        pltpu.VMEM((2, G * cap, D_MODEL), F32),
                        pltpu.SemaphoreType.DMA((G * S // cap,)),
                        pltpu.SemaphoreType.DMA(())],
    )
    return pl.pallas_call(
        functools.partial(_moe_kernel, S=S, cap=cap),
        grid_spec=grid_spec,
        out_shape=jax.ShapeDtypeStruct((T, D_MODEL), F32),
        input_output_aliases={1: 0},
        compiler_params=pltpu.CompilerParams(dimension_semantics=("arbitrary",) * 3,
                                             vmem_limit_bytes=VMEM_LIMIT),
        name="moe",
    )(idx_flat, x1, gates, wg, wu, wd, ng)


def _t5_bucket(rel):
    nb = N_BUCKETS // 2
    max_exact = nb // 2
    ret = (rel > 0).astype(np.int32) * nb
    n = np.abs(rel)
    large = max_exact + (np.log(np.maximum(n, 1) / max_exact) / np.log(MAX_DISTANCE / max_exact)
                         * (nb - max_exact)).astype(np.int32)
    large = np.minimum(large, nb - 1)
    return (ret + np.where(n < max_exact, n, large)).astype(np.int32)


def _bias_kernel(table_ref, bucket_ref, out_ref):
    bucket = bucket_ref[...]
    row = lax.broadcasted_iota(jnp.int32, (BLOCK, SPAN), 0)
    col = lax.broadcasted_iota(jnp.int32, (BLOCK, SPAN), 1)
    band = jnp.abs(col - WINDOW - row) <= WINDOW
    valid = (band & (col >= WINDOW), band, band & (col < WINDOW + BLOCK))
    for h in range(N_HEADS):
        b = jnp.zeros((BLOCK, SPAN), F32)
        for n in range(N_BUCKETS):
            b = jnp.where(bucket == n, table_ref[n, h], b)
        kh, slab, half = h // Q_GROUP, (h % Q_GROUP) // 2, h % 2
        for var in range(3):
            out_ref[var, kh, slab * BLOCK:(slab + 1) * BLOCK, half * SPAN:(half + 1) * SPAN] = (
                jnp.where(valid[var], b, -jnp.inf))


def _attention_bias(rel_table):
    rel = (np.arange(SPAN)[None, :] - WINDOW) - np.arange(BLOCK)[:, None]
    bucket = jnp.asarray(_t5_bucket(rel))
    return pl.pallas_call(
        _bias_kernel,
        in_specs=[pl.BlockSpec(memory_space=pltpu.SMEM),
                  pl.BlockSpec(bucket.shape, lambda: (0, 0))],
        out_specs=pl.BlockSpec((3, N_KV_HEADS, 2 * BLOCK, 2 * SPAN), lambda: (0, 0, 0, 0)),
        out_shape=jax.ShapeDtypeStruct((3, N_KV_HEADS, 2 * BLOCK, 2 * SPAN), F32),
        name="t5bias",
    )(rel_table.astype(F32), bucket)


def _dup_heads(w):
    parts = []
    for h in range(N_KV_HEADS):
        wh = w[..., h * HEAD_DIM:(h + 1) * HEAD_DIM]
        parts += [wh, wh]
    return jnp.concatenate(parts, axis=-1)


def _mean_matrix(width):
    blk = np.arange(width) // HEAD_DIM
    return jnp.asarray((blk[:, None] == blk[None, :]).astype(np.float32) / HEAD_DIM, dtype=BF16)


def kernel(x, norm_mix_g, w_in, b_gate, vnorm_g, w_spatial, b_spatial, q_norm_g, k_norm_g,
           attn_sink, rel_bias_table, w_proj_a, w_proj_b, w_out, norm_ffn_g, w_router,
           w_gate_e, w_up_e, w_down_e):
    B, S, _ = x.shape
    T = B * S
    cap = CAPACITY_FACTOR * S // N_EXPERTS
    assert S % TQ_MIX == 0 and T % TM_INPROJ == 0 and B % MOE_GROUP == 0
    assert B % min(TOPK_BATCHES, B) == 0
    xf = x.reshape(T, D_MODEL)
    l = 0

    w = w_in[l]
    o_q = 2 * A_WIDTH
    o_k = o_q + B_WIDTH
    o_v = o_k + KV_WIDTH
    o_g = o_v + KV_WIDTH
    w_all = jnp.concatenate([w[:, :o_k], _dup_heads(w[:, o_k:o_v]), _dup_heads(w[:, o_v:o_g]),
                             w[:, o_g:]], axis=1).astype(BF16)
    qg = jnp.tile(q_norm_g[l], N_HEADS)[None, :] * (HEAD_DIM ** -0.5)
    kg = jnp.tile(k_norm_g[l], 2 * N_KV_HEADS)[None, :]
    wsp = w_spatial[l].reshape(A_GROUPS // 2, 2, CHUNK, CHUNK).transpose(0, 2, 1, 3)
    wsp = wsp.reshape(A_GROUPS // 2, CHUNK, 2 * CHUNK).astype(BF16)
    bsp = jnp.repeat(b_spatial[l].T, A_GROUP_DIM, axis=1)
    bias = _attention_bias(rel_bias_table)
    sink = attn_sink[l].astype(F32).reshape(N_KV_HEADS, 2, 2)
    sink = jnp.broadcast_to(sink.transpose(0, 2, 1)[:, :, :, None, None],
                            (N_KV_HEADS, 2, 2, BLOCK, 1)).reshape(N_KV_HEADS, 2, 2 * BLOCK, 1)
    wr = w_router[l]
    wrh = wr.astype(BF16)
    wrl = (wr - wrh.astype(F32)).astype(BF16)
    pad = jnp.zeros((D_MODEL, LANES - 3 * N_EXPERTS), BF16)
    wr2 = jnp.concatenate([wrh, wrh, wrh, pad, wrl, wrl, wrl, pad], axis=1)
    tri = jnp.asarray(np.triu(np.ones((LANES, LANES), np.float32), k=1), dtype=BF16)

    u, vn, q, k, v, ga, gb = _inproj(xf, norm_mix_g[l][None, :], w_all, b_gate[l][None, :],
                                     vnorm_g[l][None, :], qg, kg,
                                     _mean_matrix(B_WIDTH), _mean_matrix(KV_DUP))
    x1, aff, val = _mix(B, S, u, vn, q, k, v, ga, gb, xf, wsp, bsp, bias, sink,
                        w_proj_a[l].astype(BF16), w_proj_b[l].astype(BF16), w_out[l].astype(BF16),
                        norm_ffn_g[l][None, :], wr2)
    afft = aff.reshape(B, S, N_EXPERTS).transpose(0, 2, 1).reshape(B * N_EXPERTS, S)
    idx, gates = _topk(B, S, cap, afft, val.reshape(B, S, LANES), tri)
    out = _moe(B, S, cap, idx.reshape(-1), x1, gates.reshape(B, N_EXPERTS, cap, 1),
               w_gate_e[l].astype(BF16), w_up_e[l].astype(BF16), w_down_e[l].astype(BF16),
               norm_ffn_g[l][None, :])
    return out.reshape(B, S, D_MODEL)
```

```python
import functools

import numpy as np
import jax
import jax.numpy as jnp
from jax import lax
from jax.experimental import pallas as pl
from jax.experimental.pallas import tpu as pltpu

F32 = jnp.float32
BF16 = jnp.bfloat16

D_MODEL = 1024
A_GROUPS = 8
A_GROUP_DIM = 64
A_WIDTH = A_GROUPS * A_GROUP_DIM
CHUNK = 128
N_HEADS = 8
N_KV_HEADS = 2
HEAD_DIM = 64
Q_GROUP = N_HEADS // N_KV_HEADS
B_WIDTH = N_HEADS * HEAD_DIM
KV_WIDTH = N_KV_HEADS * HEAD_DIM
WINDOW = 128
BLOCK = 128
SPAN = BLOCK + 2 * WINDOW
N_BUCKETS = 32
MAX_DISTANCE = 128
N_EXPERTS = 16
EXPERT_FF = 2048
CAPACITY_FACTOR = 2
EPS = 1e-6

LANES = 128
KV_DUP = 2 * KV_WIDTH
TM_INPROJ = 512
TQ_MIX = 512
FF_CHUNK = 256
MOE_GROUP = 2
SCATTER_ROWS = 8
TOPK_BATCHES = 8
TOPK_ROWS = 8
VMEM_LIMIT = 56 * 1024 * 1024
INF_BITS = 0x7F800000


def _rms(x, g):
    return x * lax.rsqrt(jnp.mean(x * x, axis=-1, keepdims=True) + EPS) * g


def _inproj_kernel(x_ref, g_ref, w_ref, bg_ref, vg_ref, qg_ref, kg_ref, bdq_ref, bdk_ref,
                   u_ref, vn_ref, q_ref, k_ref, v_ref, ga_ref, gb_ref):
    hb = _rms(x_ref[...], g_ref[...]).astype(BF16)

    def seg(lo, hi):
        return jnp.dot(hb, w_ref[:, lo:hi], preferred_element_type=F32)

    c0 = 0
    u_ref[...] = jax.nn.gelu(seg(c0, c0 + A_WIDTH)).astype(BF16)
    c0 += A_WIDTH
    gv = jax.nn.gelu(seg(c0, c0 + A_WIDTH))
    vn_ref[...] = _rms(gv, vg_ref[...]).astype(BF16)
    c0 += A_WIDTH
    zq = seg(c0, c0 + B_WIDTH)
    msq = jnp.dot((zq * zq).astype(BF16), bdq_ref[...], preferred_element_type=F32)
    q_ref[...] = (zq * lax.rsqrt(msq + EPS) * qg_ref[...]).astype(BF16)
    c0 += B_WIDTH
    zk = seg(c0, c0 + KV_DUP)
    msk = jnp.dot((zk * zk).astype(BF16), bdk_ref[...], preferred_element_type=F32)
    k_ref[...] = (zk * lax.rsqrt(msk + EPS) * kg_ref[...]).astype(BF16)
    c0 += KV_DUP
    v_ref[...] = seg(c0, c0 + KV_DUP).astype(BF16)
    c0 += KV_DUP
    ga_ref[...] = jax.nn.sigmoid(seg(c0, c0 + D_MODEL) + bg_ref[:, :D_MODEL]).astype(BF16)
    c0 += D_MODEL
    gb_ref[...] = jax.nn.sigmoid(seg(c0, c0 + D_MODEL) + bg_ref[:, D_MODEL:]).astype(BF16)


def _inproj(xf, norm_g, w_all, b_gate, vnorm_g, qg, kg, bdq, bdk):
    T = xf.shape[0]
    tm = TM_INPROJ
    ncol = w_all.shape[1]
    row = lambda w: pl.BlockSpec((tm, w), lambda i: (i, 0))
    full = lambda a: pl.BlockSpec(a.shape, lambda i: (0,) * a.ndim)
    out_w = (A_WIDTH, A_WIDTH, B_WIDTH, KV_DUP, KV_DUP, D_MODEL, D_MODEL)
    return pl.pallas_call(
        _inproj_kernel,
        grid=(T // tm,),
        in_specs=[row(D_MODEL), full(norm_g), pl.BlockSpec((D_MODEL, ncol), lambda i: (0, 0)),
                  full(b_gate), full(vnorm_g), full(qg), full(kg), full(bdq), full(bdk)],
        out_specs=[row(w) for w in out_w],
        out_shape=[jax.ShapeDtypeStruct((T, w), BF16) for w in out_w],
        compiler_params=pltpu.CompilerParams(dimension_semantics=("arbitrary",),
                                             vmem_limit_bytes=VMEM_LIMIT),
        name="inproj",
    )(xf, norm_g, w_all, b_gate, vnorm_g, qg, kg, bdq, bdk)


def _block_diag_pair(slab, lo_mask):
    zero = jnp.zeros_like(slab)
    return jnp.concatenate([jnp.where(lo_mask, slab, zero), jnp.where(lo_mask, zero, slab)], axis=0)


def _mix_kernel(u_ref, vn_ref, q_ref, kc_ref, kp_ref, kn_ref, vc_ref, vp_ref, vx_ref,
                ga_ref, gb_ref, x_ref, wsp_ref, bsp_ref, bias_ref, sink_ref,
                wpa_ref, wpb_ref, wo_ref, ng_ref, wr_ref,
                x1_ref, aff_ref, kbuf, vbuf, abuf, obuf):
    tq = x_ref.shape[0]
    nblk = tq // BLOCK
    i = pl.program_id(1)
    last = pl.num_programs(1) - 1
    lo_mask = lax.broadcasted_iota(jnp.int32, (1, LANES), 1) < HEAD_DIM

    kbuf[0:WINDOW, :] = kp_ref[...]
    kbuf[WINDOW:WINDOW + tq, :] = kc_ref[...]
    kbuf[WINDOW + tq:, :] = kn_ref[...]
    vbuf[0:WINDOW, :] = vp_ref[...]
    vbuf[WINDOW:WINDOW + tq, :] = vc_ref[...]
    vbuf[WINDOW + tq:, :] = vx_ref[...]

    for c in range(tq // CHUNK):
        rows = slice(c * CHUNK, (c + 1) * CHUNK)
        parts = []
        for j in range(A_GROUPS // 2):
            bd = _block_diag_pair(vn_ref[rows, j * LANES:(j + 1) * LANES], lo_mask)
            parts.append(jnp.dot(wsp_ref[j], bd, preferred_element_type=F32))
        mixed = jnp.concatenate(parts, axis=1) + bsp_ref[...]
        abuf[rows, :] = (u_ref[rows, :].astype(F32) * mixed).astype(BF16)

    vrow = lax.broadcasted_iota(jnp.int32, (2 * SPAN, LANES), 0)
    vcol = lax.broadcasted_iota(jnp.int32, (2 * SPAN, LANES), 1)
    ones_cols = jnp.where(vcol == vrow // SPAN, 1.0, 0.0).astype(BF16)

    def attention(n, kh):
        rows = slice(n * BLOCK, (n + 1) * BLOCK)
        if n == 0:
            var = jnp.where(i == 0, 0, 1)
        elif n == nblk - 1:
            var = jnp.where(i == last, 2, 1)
        else:
            var = 1
        span = slice(n * BLOCK, n * BLOCK + SPAN)
        ksl = slice(kh * LANES, (kh + 1) * LANES)
        kk = _block_diag_pair(kbuf[span, ksl], lo_mask)
        vv = jnp.concatenate([_block_diag_pair(vbuf[span, ksl], lo_mask), ones_cols], axis=1)
        q2 = jnp.concatenate([q_ref[rows, (2 * kh) * LANES:(2 * kh + 1) * LANES],
                              q_ref[rows, (2 * kh + 1) * LANES:(2 * kh + 2) * LANES]], axis=0)
        s = lax.dot_general(q2, kk, (((1,), (1,)), ((), ())), preferred_element_type=F32)
        s = s + bias_ref[var, kh]
        ps, ms = [], []
        for half in range(2):
            sh = s[:, half * SPAN:(half + 1) * SPAN]
            m = jnp.maximum(jnp.max(sh, axis=-1, keepdims=True), sink_ref[kh, half])
            ps.append(jnp.exp(sh - m).astype(BF16))
            ms.append(m)
        o2 = jnp.dot(jnp.concatenate(ps, axis=1), vv, preferred_element_type=F32)
        rs = [1.0 / (o2[:, LANES + half:LANES + half + 1] + jnp.exp(sink_ref[kh, half] - ms[half]))
              for half in range(2)]
        o = o2[:, :LANES] * jnp.where(lo_mask, rs[0], rs[1])
        obuf[rows, (2 * kh) * LANES:(2 * kh + 1) * LANES] = o[:BLOCK].astype(BF16)
        obuf[rows, (2 * kh + 1) * LANES:(2 * kh + 2) * LANES] = o[BLOCK:].astype(BF16)

    def projection_pieces(rows):
        st = {}

        def branches():
            pa = jnp.dot(abuf[rows, :], wpa_ref[...], preferred_element_type=F32)
            pb = jnp.dot(obuf[rows, :], wpb_ref[...], preferred_element_type=F32)
            st["merged"] = (ga_ref[rows, :].astype(F32) * pa + gb_ref[rows, :].astype(F32) * pb).astype(BF16)

        def output():
            st["x1"] = x_ref[rows, :] + jnp.dot(st["merged"], wo_ref[...], preferred_element_type=F32)
            x1_ref[rows, :] = st["x1"]

        def router():
            h2 = _rms(st["x1"], ng_ref[...])
            hi = h2.astype(BF16)
            lo = (h2 - hi.astype(F32)).astype(BF16)
            both = jnp.dot(hi, wr_ref[...], preferred_element_type=F32)
            logits = (both[:, :LANES] + jnp.dot(lo, wr_ref[:, :LANES], preferred_element_type=F32)
                      + both[:, LANES:])[:, :N_EXPERTS]
            ex = jnp.exp(logits - jnp.max(logits, axis=-1, keepdims=True))
            aff_ref[rows, :] = ex / jnp.sum(ex, axis=-1, keepdims=True)

        return [branches, output, router]

    units = [(n, kh) for n in range(nblk) for kh in range(N_KV_HEADS)]
    first, second = units[:len(units) // 2], units[len(units) // 2:]
    for n, kh in first:
        attention(n, kh)
    pieces = projection_pieces(slice(0, tq // 2))
    for k, (n, kh) in enumerate(second):
        attention(n, kh)
        if k < len(pieces):
            pieces[k]()
    for piece in pieces[len(second):] + projection_pieces(slice(tq // 2, tq)):
        piece()


def _mix(B, S, u, vn, q, k, v, ga, gb, xf, wsp, bsp, bias, sink, wpa, wpb, wo, ng, wr2):
    T = B * S
    tq = TQ_MIX
    nq = S // tq
    bpq = tq // BLOCK
    nb = S // BLOCK
    row = lambda w: pl.BlockSpec((tq, w), lambda b, i: (b * nq + i, 0))
    prev = pl.BlockSpec((BLOCK, KV_DUP), lambda b, i: (b * nb + jnp.maximum(i * bpq - 1, 0), 0))
    nxt = pl.BlockSpec((BLOCK, KV_DUP), lambda b, i: (b * nb + jnp.minimum(i * bpq + bpq, nb - 1), 0))
    full = lambda a: pl.BlockSpec(a.shape, lambda b, i: (0,) * a.ndim)
    return pl.pallas_call(
        _mix_kernel,
        grid=(B, nq),
        in_specs=[row(A_WIDTH), row(A_WIDTH), row(B_WIDTH),
                  row(KV_DUP), prev, nxt, row(KV_DUP), prev, nxt,
                  row(D_MODEL), row(D_MODEL), row(D_MODEL),
                  full(wsp), full(bsp), full(bias), full(sink),
                  full(wpa), full(wpb), full(wo), full(ng), full(wr2)],
        out_specs=[row(D_MODEL), row(N_EXPERTS)],
        out_shape=[jax.ShapeDtypeStruct((T, D_MODEL), F32),
                   jax.ShapeDtypeStruct((T, N_EXPERTS), F32)],
        scratch_shapes=[pltpu.VMEM((tq + 2 * WINDOW, KV_DUP), BF16),
                        pltpu.VMEM((tq + 2 * WINDOW, KV_DUP), BF16),
                        pltpu.VMEM((tq, A_WIDTH), BF16),
                        pltpu.VMEM((tq, B_WIDTH), BF16)],
        compiler_params=pltpu.CompilerParams(dimension_semantics=("arbitrary", "arbitrary"),
                                             vmem_limit_bytes=VMEM_LIMIT),
        name="mix",
    )(u, vn, q, k, k, k, v, v, v, ga, gb, xf, wsp, bsp, bias, sink, wpa, wpb, wo, ng, wr2)


def _topk_kernel(afft_ref, tri_ref, idx_ref, gate_ref, cum, cend, *, cap):
    R, S = afft_ref.shape
    nblk = S // LANES
    j = pl.program_id(1)
    lane = lax.broadcasted_iota(jnp.int32, (1, LANES), 1)

    @pl.when(j == 0)
    def _():
        bits = pltpu.bitcast(afft_ref[...], jnp.int32)

        def search(_, carry):
            lo, hi = carry
            mid = lo + ((hi - lo) >> 1)
            cnt = jnp.sum(jnp.where(bits >= mid, 1.0, 0.0), axis=1, keepdims=True)
            ge = cnt >= cap
            return jnp.where(ge, mid, lo), jnp.where(ge, hi, mid)

        lo0 = jnp.zeros((R, 1), jnp.int32)
        hi0 = jnp.full((R, 1), INF_BITS, jnp.int32)
        thr, _ = lax.fori_loop(0, 31, search, (lo0, hi0))

        def prefix(mask):
            outs, carry, ends = [], jnp.zeros((R, 1), F32), jnp.zeros((R, LANES), F32)
            for jj in range(nblk):
                mj = mask[:, jj * LANES:(jj + 1) * LANES]
                outs.append(jnp.dot(mj.astype(BF16), tri_ref[...], preferred_element_type=F32) + carry)
                carry = carry + jnp.sum(mj, axis=1, keepdims=True)
                ends = jnp.where(lane == jj, carry, ends)
            return jnp.concatenate(outs, axis=1), ends

        gt = jnp.where(bits > thr, 1.0, 0.0)
        eq = jnp.where(bits == thr, 1.0, 0.0)
        need = cap - jnp.sum(gt, axis=1, keepdims=True)
        sel = gt + eq * jnp.where(prefix(eq)[0] < need, 1.0, 0.0)
        before, ends = prefix(sel)
        cum[...] = before + sel
        cend[...] = ends

    slot = lax.broadcasted_iota(jnp.int32, (cap, 1), 0).astype(F32)

    def by_block(row):
        return jnp.concatenate([row[:, b * LANES:(b + 1) * LANES] for b in range(nblk)], axis=0)

    def compact(rr):
        r = j * TOPK_ROWS + rr
        done = jnp.where((cend[pl.ds(r, 1), :] <= slot) & (lane < nblk), 1.0, 0.0)
        blk = jnp.sum(done, axis=1, keepdims=True).astype(jnp.int32)
        pick = jnp.where(lane == blk, 1.0, 0.0).astype(BF16)[:, :nblk]
        cum_b = jnp.dot(pick, by_block(cum[pl.ds(r, 1), :]).astype(BF16), preferred_element_type=F32)
        off = jnp.sum(jnp.where(cum_b <= slot, 1.0, 0.0), axis=1, keepdims=True).astype(jnp.int32)
        idx_ref[rr] = blk * LANES + off
        a = by_block(afft_ref[pl.ds(r, 1), :])
        a_hi = a.astype(BF16)
        r1 = a - a_hi.astype(F32)
        a_mid = r1.astype(BF16)
        a_lo = (r1 - a_mid.astype(F32)).astype(BF16)
        aff_b = (jnp.dot(pick, a_hi, preferred_element_type=F32)
                 + jnp.dot(pick, a_mid, preferred_element_type=F32)
                 + jnp.dot(pick, a_lo, preferred_element_type=F32))
        gate_ref[rr] = jnp.sum(jnp.where(lane == off, aff_b, 0.0), axis=1, keepdims=True)

    for rr in range(TOPK_ROWS):
        compact(rr)


def _topk(B, S, cap, afft, tri):
    nb = min(TOPK_BATCHES, B)
    R = nb * N_EXPERTS
    steps = R // TOPK_ROWS
    out_block = pl.BlockSpec((TOPK_ROWS, cap, 1), lambda i, j: (i * steps + j, 0, 0))
    return pl.pallas_call(
        functools.partial(_topk_kernel, cap=cap),
        grid=(B // nb, steps),
        in_specs=[pl.BlockSpec((R, S), lambda i, j: (i, 0)),
                  pl.BlockSpec(tri.shape, lambda i, j: (0, 0))],
        out_specs=[out_block, out_block],
        out_shape=[jax.ShapeDtypeStruct((B * N_EXPERTS, cap, 1), jnp.int32),
                   jax.ShapeDtypeStruct((B * N_EXPERTS, cap, 1), F32)],
        scratch_shapes=[pltpu.VMEM((R, S), F32), pltpu.VMEM((R, LANES), F32)],
        compiler_params=pltpu.CompilerParams(dimension_semantics=("arbitrary", "arbitrary"),
                                             vmem_limit_bytes=VMEM_LIMIT),
        name="topk",
    )(afft, tri)


def _moe_kernel(idx_ref, x1_hbm, gate_ref, wg_ref, wu_ref, wd_ref, ng_ref, out_hbm,
                acc, hbuf, xg, ybuf, sem_in, sem, *, S, cap):
    g = pl.program_id(0)
    e2 = pl.program_id(1)
    k = pl.program_id(2)
    nexp = 2 * pl.num_programs(1)
    half = D_MODEL // 2
    rows0 = g * (MOE_GROUP * S)
    hi_mask = jnp.int32(-65536)

    def lists(seq, ee):
        return ((g * MOE_GROUP + seq) * nexp + ee) * cap

    def gather_jobs(seq, ee, par):
        base = lists(seq, ee)

        def job(c0):
            for c in range(c0, c0 + SCATTER_ROWS):
                r = seq * S + idx_ref[base + c]
                xg[par, seq * cap + c:seq * cap + c + 1, :] = hbuf[pl.ds(r, 1), :]
        return [functools.partial(job, c0) for c0 in range(0, cap, SCATTER_ROWS)]

    def scatter_jobs(seq, ee, par):
        base = lists(seq, ee)

        def job(c0):
            rows = [seq * S + idx_ref[base + c] for c in range(c0, c0 + SCATTER_ROWS)]
            old = [acc[pl.ds(r, 1), :] for r in rows]
            for j, r in enumerate(rows):
                c = seq * cap + c0 + j
                acc[pl.ds(r, 1), :] = old[j] + ybuf[par, c:c + 1, :]
        return [functools.partial(job, c0) for c0 in range(0, cap, SCATTER_ROWS)]

    def ffn(par, accumulate, jobs):
        pieces = 2 * (wg_ref.shape[2] // FF_CHUNK) if jobs else 1
        share = -(-len(jobs) // pieces)
        jobs = list(jobs)

        def run_share():
            for job in jobs[:share]:
                job()
            del jobs[:share]

        p = xg[par]
        lo = pltpu.bitcast(p << 16, F32).astype(BF16)
        hi = pltpu.bitcast(p & hi_mask, F32).astype(BF16)
        x = jnp.concatenate([lo, hi], axis=1)
        hid = []
        for f in range(wg_ref.shape[2] // FF_CHUNK):
            cols = slice(f * FF_CHUNK, (f + 1) * FF_CHUNK)
            gt = jnp.dot(x, wg_ref[0, :, cols], preferred_element_type=F32)
            up = jnp.dot(x, wu_ref[0, :, cols], preferred_element_type=F32)
            hid.append((gt * jax.nn.sigmoid(gt) * up).astype(BF16))
            run_share()
        hid = jnp.concatenate(hid, axis=1)
        gate = jnp.concatenate([gate_ref[s, 0] for s in range(MOE_GROUP)], axis=0)
        for n in range(D_MODEL // FF_CHUNK):
            cols = slice(n * FF_CHUNK, (n + 1) * FF_CHUNK)
            y = jnp.dot(hid, wd_ref[0, :, cols], preferred_element_type=F32) * gate
            ybuf[par, :, cols] = ybuf[par, :, cols] + y if accumulate else y
            run_share()
        assert not jobs

    @pl.when((e2 == 0) & (k == 0))
    def _():
        def load(i):
            start = i * cap if isinstance(i, int) else pl.multiple_of(i * cap, cap)
            return pltpu.make_async_copy(x1_hbm.at[pl.ds(rows0 + start, cap), :],
                                         acc.at[pl.ds(start, cap), :], sem_in.at[i])

        for i in range(sem_in.shape[0]):
            load(i).start()

        def pack(i, _):
            load(i).wait()
            rows = pl.ds(pl.multiple_of(i * cap, cap), cap)
            hb = _rms(acc[rows, :], ng_ref[...]).astype(BF16).astype(F32)
            u = pltpu.bitcast(hb, jnp.int32)
            hbuf[rows, :] = lax.shift_right_logical(u[:, :half], 16) | (u[:, half:] & hi_mask)
            return 0
        lax.fori_loop(0, MOE_GROUP * S // cap, pack, 0)
        ybuf[1] = jnp.zeros(ybuf.shape[1:], F32)
        for seq in range(MOE_GROUP):
            for job in gather_jobs(seq, 0, 0):
                job()

    for kk in range(4):
        @pl.when(k == kk)
        def _(par=kk // 2, fh=kk % 2):
            e = 2 * e2 + par
            sc = scatter_jobs(fh, jnp.maximum(e - 1, 0), 1 - par)
            ga = gather_jobs(fh, jnp.minimum(e + 1, nexp - 1), 1 - par)
            ffn(par, fh == 1, [job for pair in zip(sc, ga) for job in pair])

    @pl.when((e2 == pl.num_programs(1) - 1) & (k == 3))
    def _():
        for seq in range(MOE_GROUP):
            for job in scatter_jobs(seq, nexp - 1, 1):
                job()
        cp = pltpu.make_async_copy(acc, out_hbm.at[pl.ds(rows0, MOE_GROUP * S), :], sem)
        cp.start()
        cp.wait()


def _moe(B, S, cap, idx_flat, x1, gates, wg, wu, wd, ng):
    G = MOE_GROUP
    T = B * S
    fhw = EXPERT_FF // 2
    ex = lambda e2, k: 2 * e2 + k // 2
    fh = lambda k: k % 2
    grid_spec = pltpu.PrefetchScalarGridSpec(
        num_scalar_prefetch=1,
        grid=(B // G, N_EXPERTS // 2, 4),
        in_specs=[pl.BlockSpec(memory_space=pl.ANY),
                  pl.BlockSpec((G, 1, cap, 1), lambda g, e2, k, idx: (g, ex(e2, k), 0, 0)),
                  pl.BlockSpec((1, D_MODEL, fhw), lambda g, e2, k, idx: (ex(e2, k), 0, fh(k))),
                  pl.BlockSpec((1, D_MODEL, fhw), lambda g, e2, k, idx: (ex(e2, k), 0, fh(k))),
                  pl.BlockSpec((1, fhw, D_MODEL), lambda g, e2, k, idx: (ex(e2, k), fh(k), 0)),
                  pl.BlockSpec((1, D_MODEL), lambda g, e2, k, idx: (0, 0))],
        out_specs=pl.BlockSpec(memory_space=pl.ANY),
        scratch_shapes=[pltpu.VMEM((G * S, D_MODEL), F32),
                        pltpu.VMEM((G * S, D_MODEL // 2), jnp.int32),
                        pltpu.VMEM((2, G * cap, D_MODEL // 2), jnp.int32),
                        pltpu.VMEM((2, G * cap, D_MODEL), F32),
                        pltpu.SemaphoreType.DMA((G * S // cap,)),
                        pltpu.SemaphoreType.DMA(())],
    )
    return pl.pallas_call(
        functools.partial(_moe_kernel, S=S, cap=cap),
        grid_spec=grid_spec,
        out_shape=jax.ShapeDtypeStruct((T, D_MODEL), F32),
        input_output_aliases={1: 0},
        compiler_params=pltpu.CompilerParams(dimension_semantics=("arbitrary",) * 3,
                                             vmem_limit_bytes=VMEM_LIMIT),
        name="moe",
    )(idx_flat, x1, gates, wg, wu, wd, ng)


def _t5_bucket(rel):
    nb = N_BUCKETS // 2
    max_exact = nb // 2
    ret = (rel > 0).astype(np.int32) * nb
    n = np.abs(rel)
    large = max_exact + (np.log(np.maximum(n, 1) / max_exact) / np.log(MAX_DISTANCE / max_exact)
                         * (nb - max_exact)).astype(np.int32)
    large = np.minimum(large, nb - 1)
    return (ret + np.where(n < max_exact, n, large)).astype(np.int32)


def _bias_kernel(table_ref, bucket_ref, out_ref):
    bucket = bucket_ref[...]
    row = lax.broadcasted_iota(jnp.int32, (BLOCK, SPAN), 0)
    col = lax.broadcasted_iota(jnp.int32, (BLOCK, SPAN), 1)
    band = jnp.abs(col - WINDOW - row) <= WINDOW
    valid = (band & (col >= WINDOW), band, band & (col < WINDOW + BLOCK))
    for h in range(N_HEADS):
        b = jnp.zeros((BLOCK, SPAN), F32)
        for n in range(N_BUCKETS):
            b = jnp.where(bucket == n, table_ref[n, h], b)
        kh, slab, half = h // Q_GROUP, (h % Q_GROUP) // 2, h % 2
        for var in range(3):
            out_ref[var, kh, slab * BLOCK:(slab + 1) * BLOCK, half * SPAN:(half + 1) * SPAN] = (
                jnp.where(valid[var], b, -jnp.inf))


def _attention_bias(rel_table):
    rel = (np.arange(SPAN)[None, :] - WINDOW) - np.arange(BLOCK)[:, None]
    bucket = jnp.asarray(_t5_bucket(rel))
    return pl.pallas_call(
        _bias_kernel,
        in_specs=[pl.BlockSpec(memory_space=pltpu.SMEM),
                  pl.BlockSpec(bucket.shape, lambda: (0, 0))],
        out_specs=pl.BlockSpec((3, N_KV_HEADS, 2 * BLOCK, 2 * SPAN), lambda: (0, 0, 0, 0)),
        out_shape=jax.ShapeDtypeStruct((3, N_KV_HEADS, 2 * BLOCK, 2 * SPAN), F32),
        name="t5bias",
    )(rel_table.astype(F32), bucket)


def _dup_heads(w):
    parts = []
    for h in range(N_KV_HEADS):
        wh = w[..., h * HEAD_DIM:(h + 1) * HEAD_DIM]
        parts += [wh, wh]
    return jnp.concatenate(parts, axis=-1)


def _mean_matrix(width):
    blk = np.arange(width) // HEAD_DIM
    return jnp.asarray((blk[:, None] == blk[None, :]).astype(np.float32) / HEAD_DIM, dtype=BF16)


def kernel(x, norm_mix_g, w_in, b_gate, vnorm_g, w_spatial, b_spatial, q_norm_g, k_norm_g,
           attn_sink, rel_bias_table, w_proj_a, w_proj_b, w_out, norm_ffn_g, w_router,
           w_gate_e, w_up_e, w_down_e):
    B, S, _ = x.shape
    T = B * S
    cap = CAPACITY_FACTOR * S // N_EXPERTS
    assert S % TQ_MIX == 0 and T % TM_INPROJ == 0 and B % MOE_GROUP == 0
    assert B % min(TOPK_BATCHES, B) == 0
    xf = x.reshape(T, D_MODEL)
    l = 0

    w = w_in[l]
    o_q = 2 * A_WIDTH
    o_k = o_q + B_WIDTH
    o_v = o_k + KV_WIDTH
    o_g = o_v + KV_WIDTH
    w_all = jnp.concatenate([w[:, :o_k], _dup_heads(w[:, o_k:o_v]), _dup_heads(w[:, o_v:o_g]),
                             w[:, o_g:]], axis=1).astype(BF16)
    qg = jnp.tile(q_norm_g[l], N_HEADS)[None, :] * (HEAD_DIM ** -0.5)
    kg = jnp.tile(k_norm_g[l], 2 * N_KV_HEADS)[None, :]
    wsp = w_spatial[l].reshape(A_GROUPS // 2, 2, CHUNK, CHUNK).transpose(0, 2, 1, 3)
    wsp = wsp.reshape(A_GROUPS // 2, CHUNK, 2 * CHUNK).astype(BF16)
    bsp = jnp.repeat(b_spatial[l].T, A_GROUP_DIM, axis=1)
    bias = _attention_bias(rel_bias_table)
    sink = attn_sink[l].astype(F32).reshape(N_KV_HEADS, 2, 2)
    sink = jnp.broadcast_to(sink.transpose(0, 2, 1)[:, :, :, None, None],
                            (N_KV_HEADS, 2, 2, BLOCK, 1)).reshape(N_KV_HEADS, 2, 2 * BLOCK, 1)
    wr = w_router[l]
    wrh = wr.astype(BF16)
    wrl = (wr - wrh.astype(F32)).astype(BF16)
    pad = jnp.zeros((D_MODEL, LANES - N_EXPERTS), BF16)
    wr2 = jnp.concatenate([wrh, pad, wrl, pad], axis=1)
    tri = jnp.asarray(np.triu(np.ones((LANES, LANES), np.float32), k=1), dtype=BF16)

    u, vn, q, k, v, ga, gb = _inproj(xf, norm_mix_g[l][None, :], w_all, b_gate[l][None, :],
                                     vnorm_g[l][None, :], qg, kg,
                                     _mean_matrix(B_WIDTH), _mean_matrix(KV_DUP))
    x1, aff = _mix(B, S, u, vn, q, k, v, ga, gb, xf, wsp, bsp, bias, sink,
                        w_proj_a[l].astype(BF16), w_proj_b[l].astype(BF16), w_out[l].astype(BF16),
                        norm_ffn_g[l][None, :], wr2)
    afft = aff.reshape(B, S, N_EXPERTS).transpose(0, 2, 1).reshape(B * N_EXPERTS, S)
    idx, gates = _topk(B, S, cap, afft, tri)
    out = _moe(B, S, cap, idx.reshape(-1), x1, gates.reshape(B, N_EXPERTS, cap, 1),
               w_gate_e[l].astype(BF16), w_up_e[l].astype(BF16), w_down_e[l].astype(BF16),
               norm_ffn_g[l][None, :])
    return out.reshape(B, S, D_MODEL)
```

```python
import functools

import numpy as np
import jax
import jax.numpy as jnp
from jax import lax
from jax.experimental import pallas as pl
from jax.experimental.pallas import tpu as pltpu

F32 = jnp.float32
BF16 = jnp.bfloat16

D_MODEL = 1024
A_GROUPS = 8
A_GROUP_DIM = 64
A_WIDTH = A_GROUPS * A_GROUP_DIM
CHUNK = 128
N_HEADS = 8
N_KV_HEADS = 2
HEAD_DIM = 64
Q_GROUP = N_HEADS // N_KV_HEADS
B_WIDTH = N_HEADS * HEAD_DIM
KV_WIDTH = N_KV_HEADS * HEAD_DIM
WINDOW = 128
BLOCK = 128
SPAN = BLOCK + 2 * WINDOW
N_BUCKETS = 32
MAX_DISTANCE = 128
N_EXPERTS = 16
EXPERT_FF = 2048
CAPACITY_FACTOR = 2
EPS = 1e-6

LANES = 128
KV_DUP = 2 * KV_WIDTH
TM_INPROJ = 512
TQ_MIX = 512
FF_CHUNK = 256
MOE_GROUP = 2
SCATTER_ROWS = 8
TOPK_BATCHES = 8
TOPK_ROWS = 8
VMEM_LIMIT = 56 * 1024 * 1024
INF_BITS = 0x7F800000


def _rms(x, g):
    return x * lax.rsqrt(jnp.mean(x * x, axis=-1, keepdims=True) + EPS) * g


def _inproj_kernel(x_ref, g_ref, w_ref, bg_ref, vg_ref, qg_ref, kg_ref, bdq_ref, bdk_ref,
                   ewg_ref, ewu_ref, ewd_ref,
                   u_ref, vn_ref, q_ref, k_ref, v_ref, ga_ref, gb_ref, bwg_ref, bwu_ref, bwd_ref):
    bwg_ref[...] = ewg_ref[...].astype(BF16)
    bwu_ref[...] = ewu_ref[...].astype(BF16)
    bwd_ref[...] = ewd_ref[...].astype(BF16)

    hb = _rms(x_ref[...], g_ref[...]).astype(BF16)

    def seg(lo, hi):
        return jnp.dot(hb, w_ref[:, lo:hi], preferred_element_type=F32)

    c0 = 0
    u_ref[...] = jax.nn.gelu(seg(c0, c0 + A_WIDTH)).astype(BF16)
    c0 += A_WIDTH
    gv = jax.nn.gelu(seg(c0, c0 + A_WIDTH))
    vn_ref[...] = _rms(gv, vg_ref[...]).astype(BF16)
    c0 += A_WIDTH
    zq = seg(c0, c0 + B_WIDTH)
    msq = jnp.dot((zq * zq).astype(BF16), bdq_ref[...], preferred_element_type=F32)
    q_ref[...] = (zq * lax.rsqrt(msq + EPS) * qg_ref[...]).astype(BF16)
    c0 += B_WIDTH
    zk = seg(c0, c0 + KV_DUP)
    msk = jnp.dot((zk * zk).astype(BF16), bdk_ref[...], preferred_element_type=F32)
    k_ref[...] = (zk * lax.rsqrt(msk + EPS) * kg_ref[...]).astype(BF16)
    c0 += KV_DUP
    v_ref[...] = seg(c0, c0 + KV_DUP).astype(BF16)
    c0 += KV_DUP
    ga_ref[...] = jax.nn.sigmoid(seg(c0, c0 + D_MODEL) + bg_ref[:, :D_MODEL]).astype(BF16)
    c0 += D_MODEL
    gb_ref[...] = jax.nn.sigmoid(seg(c0, c0 + D_MODEL) + bg_ref[:, D_MODEL:]).astype(BF16)


def _inproj(xf, norm_g, w_all, b_gate, vnorm_g, qg, kg, bdq, bdk, expert_w):
    T = xf.shape[0]
    tm = TM_INPROJ
    steps = T // tm
    ncol = w_all.shape[1]
    row = lambda w: pl.BlockSpec((tm, w), lambda i: (i, 0))
    full = lambda a: pl.BlockSpec(a.shape, lambda i: (0,) * a.ndim)
    out_w = (A_WIDTH, A_WIDTH, B_WIDTH, KV_DUP, KV_DUP, D_MODEL, D_MODEL)
    flat = [w.reshape(-1, w.shape[-1]) for w in expert_w]
    assert all(w.shape[0] % (steps * 16) == 0 for w in flat)
    slab = lambda w: pl.BlockSpec((w.shape[0] // steps, w.shape[1]), lambda i: (i, 0))
    outs = pl.pallas_call(
        _inproj_kernel,
        grid=(steps,),
        in_specs=[row(D_MODEL), full(norm_g), pl.BlockSpec((D_MODEL, ncol), lambda i: (0, 0)),
                  full(b_gate), full(vnorm_g), full(qg), full(kg), full(bdq), full(bdk)]
                 + [slab(w) for w in flat],
        out_specs=[row(w) for w in out_w] + [slab(w) for w in flat],
        out_shape=[jax.ShapeDtypeStruct((T, w), BF16) for w in out_w]
                  + [jax.ShapeDtypeStruct(w.shape, BF16) for w in flat],
        compiler_params=pltpu.CompilerParams(dimension_semantics=("arbitrary",),
                                             vmem_limit_bytes=VMEM_LIMIT),
        name="inproj",
    )(xf, norm_g, w_all, b_gate, vnorm_g, qg, kg, bdq, bdk, *flat)
    return outs[:len(out_w)], [o.reshape(w.shape) for o, w in zip(outs[len(out_w):], expert_w)]


def _block_diag_pair(slab, lo_mask):
    zero = jnp.zeros_like(slab)
    return jnp.concatenate([jnp.where(lo_mask, slab, zero), jnp.where(lo_mask, zero, slab)], axis=0)


def _mix_kernel(u_ref, vn_ref, q_ref, kc_ref, kp_ref, kn_ref, vc_ref, vp_ref, vx_ref,
                ga_ref, gb_ref, x_ref, wsp_ref, bsp_ref, bias_ref, sink_ref,
                wpa_ref, wpb_ref, wo_ref, ng_ref, wr_ref,
                x1_ref, aff_ref, kbuf, vbuf, abuf, obuf):
    tq = x_ref.shape[0]
    nblk = tq // BLOCK
    i = pl.program_id(1)
    last = pl.num_programs(1) - 1
    lo_mask = lax.broadcasted_iota(jnp.int32, (1, LANES), 1) < HEAD_DIM

    kbuf[0:WINDOW, :] = kp_ref[...]
    kbuf[WINDOW:WINDOW + tq, :] = kc_ref[...]
    kbuf[WINDOW + tq:, :] = kn_ref[...]
    vbuf[0:WINDOW, :] = vp_ref[...]
    vbuf[WINDOW:WINDOW + tq, :] = vc_ref[...]
    vbuf[WINDOW + tq:, :] = vx_ref[...]

    def spatial_gating(c):
        rows = slice(c * CHUNK, (c + 1) * CHUNK)
        parts = []
        for j in range(A_GROUPS // 2):
            bd = _block_diag_pair(vn_ref[rows, j * LANES:(j + 1) * LANES], lo_mask)
            parts.append(jnp.dot(wsp_ref[j], bd, preferred_element_type=F32))
        mixed = jnp.concatenate(parts, axis=1) + bsp_ref[...]
        abuf[rows, :] = (u_ref[rows, :].astype(F32) * mixed).astype(BF16)

    vrow = lax.broadcasted_iota(jnp.int32, (2 * SPAN, LANES), 0)
    vcol = lax.broadcasted_iota(jnp.int32, (2 * SPAN, LANES), 1)
    ones_cols = jnp.where((vrow < SPAN) == (vcol < HEAD_DIM), 1.0, 0.0).astype(BF16)

    def attention(n, kh):
        rows = slice(n * BLOCK, (n + 1) * BLOCK)
        if n == 0:
            var = jnp.where(i == 0, 0, 1)
        elif n == nblk - 1:
            var = jnp.where(i == last, 2, 1)
        else:
            var = 1
        span = slice(n * BLOCK, n * BLOCK + SPAN)
        ksl = slice(kh * LANES, (kh + 1) * LANES)
        kk = _block_diag_pair(kbuf[span, ksl], lo_mask)
        vv = jnp.concatenate([_block_diag_pair(vbuf[span, ksl], lo_mask), ones_cols], axis=1)
        q2 = jnp.concatenate([q_ref[rows, (2 * kh) * LANES:(2 * kh + 1) * LANES],
                              q_ref[rows, (2 * kh + 1) * LANES:(2 * kh + 2) * LANES]], axis=0)
        s = lax.dot_general(q2, kk, (((1,), (1,)), ((), ())), preferred_element_type=F32)
        s = s + bias_ref[var, kh]
        ps, ms = [], []
        for half in range(2):
            sh = s[:, half * SPAN:(half + 1) * SPAN]
            m = jnp.maximum(jnp.max(sh, axis=-1, keepdims=True), sink_ref[kh, half])
            ps.append(jnp.exp(sh - m).astype(BF16))
            ms.append(m)
        o2 = jnp.dot(jnp.concatenate(ps, axis=1), vv, preferred_element_type=F32)
        sink_term = jnp.where(lo_mask, jnp.exp(sink_ref[kh, 0] - ms[0]), jnp.exp(sink_ref[kh, 1] - ms[1]))
        o = o2[:, :LANES] * (1.0 / (o2[:, LANES:] + sink_term))
        obuf[rows, (2 * kh) * LANES:(2 * kh + 1) * LANES] = o[:BLOCK].astype(BF16)
        obuf[rows, (2 * kh + 1) * LANES:(2 * kh + 2) * LANES] = o[BLOCK:].astype(BF16)

    def projection_pieces(rows):
        st = {}

        def branches():
            pa = jnp.dot(abuf[rows, :], wpa_ref[...], preferred_element_type=F32)
            pb = jnp.dot(obuf[rows, :], wpb_ref[...], preferred_element_type=F32)
            st["merged"] = (ga_ref[rows, :].astype(F32) * pa + gb_ref[rows, :].astype(F32) * pb).astype(BF16)

        def output():
            st["x1"] = x_ref[rows, :] + jnp.dot(st["merged"], wo_ref[...], preferred_element_type=F32)
            x1_ref[rows, :] = st["x1"]

        def router():
            h2 = _rms(st["x1"], ng_ref[...])
            hi = h2.astype(BF16)
            lo = (h2 - hi.astype(F32)).astype(BF16)
            both = jnp.dot(hi, wr_ref[...], preferred_element_type=F32)
            logits = (both[:, :LANES] + jnp.dot(lo, wr_ref[:, :LANES], preferred_element_type=F32)
                      + both[:, LANES:])[:, :N_EXPERTS]
            ex = jnp.exp(logits - jnp.max(logits, axis=-1, keepdims=True))
            aff_ref[rows, :] = ex / jnp.sum(ex, axis=-1, keepdims=True)

        return [branches, output, router]

    for c in range(nblk):
        spatial_gating(c)
    units = [(n, kh) for n in range(nblk) for kh in range(N_KV_HEADS)]
    first, second = units[:len(units) // 2], units[len(units) // 2:]
    for n, kh in first:
        attention(n, kh)
    pieces = projection_pieces(slice(0, tq // 2))
    for k, (n, kh) in enumerate(second):
        attention(n, kh)
        if k < len(pieces):
            pieces[k]()
    for piece in pieces[len(second):] + projection_pieces(slice(tq // 2, tq)):
        piece()


def _mix(B, S, u, vn, q, k, v, ga, gb, xf, wsp, bsp, bias, sink, wpa, wpb, wo, ng, wr2):
    T = B * S
    tq = TQ_MIX
    nq = S // tq
    bpq = tq // BLOCK
    nb = S // BLOCK
    row = lambda w: pl.BlockSpec((tq, w), lambda b, i: (b * nq + i, 0))
    prev = pl.BlockSpec((BLOCK, KV_DUP), lambda b, i: (b * nb + jnp.maximum(i * bpq - 1, 0), 0))
    nxt = pl.BlockSpec((BLOCK, KV_DUP), lambda b, i: (b * nb + jnp.minimum(i * bpq + bpq, nb - 1), 0))
    full = lambda a: pl.BlockSpec(a.shape, lambda b, i: (0,) * a.ndim)
    return pl.pallas_call(
        _mix_kernel,
        grid=(B, nq),
        in_specs=[row(A_WIDTH), row(A_WIDTH), row(B_WIDTH),
                  row(KV_DUP), prev, nxt, row(KV_DUP), prev, nxt,
                  row(D_MODEL), row(D_MODEL), row(D_MODEL),
                  full(wsp), full(bsp), full(bias), full(sink),
                  full(wpa), full(wpb), full(wo), full(ng), full(wr2)],
        out_specs=[row(D_MODEL), row(N_EXPERTS)],
        out_shape=[jax.ShapeDtypeStruct((T, D_MODEL), F32),
                   jax.ShapeDtypeStruct((T, N_EXPERTS), F32)],
        scratch_shapes=[pltpu.VMEM((tq + 2 * WINDOW, KV_DUP), BF16),
                        pltpu.VMEM((tq + 2 * WINDOW, KV_DUP), BF16),
                        pltpu.VMEM((tq, A_WIDTH), BF16),
                        pltpu.VMEM((tq, B_WIDTH), BF16)],
        compiler_params=pltpu.CompilerParams(dimension_semantics=("arbitrary", "arbitrary"),
                                             vmem_limit_bytes=VMEM_LIMIT),
        name="mix",
    )(u, vn, q, k, k, k, v, v, v, ga, gb, xf, wsp, bsp, bias, sink, wpa, wpb, wo, ng, wr2)


def _topk_kernel(afft_ref, tri_ref, idx_ref, gate_ref, cum, cend, *, cap):
    R, S = afft_ref.shape
    nblk = S // LANES
    j = pl.program_id(1)
    lane = lax.broadcasted_iota(jnp.int32, (1, LANES), 1)

    @pl.when(j == 0)
    def _():
        bits = pltpu.bitcast(afft_ref[...], jnp.int32)

        def search(_, carry):
            lo, hi = carry
            mid = lo + ((hi - lo) >> 1)
            cnt = jnp.sum(jnp.where(bits >= mid, 1.0, 0.0), axis=1, keepdims=True)
            ge = cnt >= cap
            return jnp.where(ge, mid, lo), jnp.where(ge, hi, mid)

        lo0 = jnp.zeros((R, 1), jnp.int32)
        hi0 = jnp.full((R, 1), INF_BITS, jnp.int32)
        thr, _ = lax.fori_loop(0, 31, search, (lo0, hi0))

        def prefix(mask):
            outs, carry, ends = [], jnp.zeros((R, 1), F32), jnp.zeros((R, LANES), F32)
            for jj in range(nblk):
                mj = mask[:, jj * LANES:(jj + 1) * LANES]
                outs.append(jnp.dot(mj.astype(BF16), tri_ref[...], preferred_element_type=F32) + carry)
                carry = carry + jnp.sum(mj, axis=1, keepdims=True)
                ends = jnp.where(lane == jj, carry, ends)
            return jnp.concatenate(outs, axis=1), ends

        gt = jnp.where(bits > thr, 1.0, 0.0)
        eq = jnp.where(bits == thr, 1.0, 0.0)
        need = cap - jnp.sum(gt, axis=1, keepdims=True)
        sel = gt + eq * jnp.where(prefix(eq)[0] < need, 1.0, 0.0)
        before, ends = prefix(sel)
        cum[...] = before + sel
        cend[...] = ends

    slot = lax.broadcasted_iota(jnp.int32, (cap, 1), 0).astype(F32)

    def by_block(row):
        return jnp.concatenate([row[:, b * LANES:(b + 1) * LANES] for b in range(nblk)], axis=0)

    def compact(rr):
        r = j * TOPK_ROWS + rr
        done = jnp.where((cend[pl.ds(r, 1), :] <= slot) & (lane < nblk), 1.0, 0.0)
        blk = jnp.sum(done, axis=1, keepdims=True).astype(jnp.int32)
        pick = jnp.where(lane == blk, 1.0, 0.0).astype(BF16)[:, :nblk]
        cum_b = jnp.dot(pick, by_block(cum[pl.ds(r, 1), :]).astype(BF16), preferred_element_type=F32)
        off = jnp.sum(jnp.where(cum_b <= slot, 1.0, 0.0), axis=1, keepdims=True).astype(jnp.int32)
        idx_ref[rr] = blk * LANES + off
        a = by_block(afft_ref[pl.ds(r, 1), :])
        a_hi = a.astype(BF16)
        r1 = a - a_hi.astype(F32)
        a_mid = r1.astype(BF16)
        a_lo = (r1 - a_mid.astype(F32)).astype(BF16)
        aff_b = (jnp.dot(pick, a_hi, preferred_element_type=F32)
                 + jnp.dot(pick, a_mid, preferred_element_type=F32)
                 + jnp.dot(pick, a_lo, preferred_element_type=F32))
        gate_ref[rr] = jnp.sum(jnp.where(lane == off, aff_b, 0.0), axis=1, keepdims=True)

    for rr in range(TOPK_ROWS):
        compact(rr)


def _topk(B, S, cap, afft, tri):
    nb = min(TOPK_BATCHES, B)
    R = nb * N_EXPERTS
    steps = R // TOPK_ROWS
    out_block = pl.BlockSpec((TOPK_ROWS, cap, 1), lambda i, j: (i * steps + j, 0, 0))
    return pl.pallas_call(
        functools.partial(_topk_kernel, cap=cap),
        grid=(B // nb, steps),
        in_specs=[pl.BlockSpec((R, S), lambda i, j: (i, 0)),
                  pl.BlockSpec(tri.shape, lambda i, j: (0, 0))],
        out_specs=[out_block, out_block],
        out_shape=[jax.ShapeDtypeStruct((B * N_EXPERTS, cap, 1), jnp.int32),
                   jax.ShapeDtypeStruct((B * N_EXPERTS, cap, 1), F32)],
        scratch_shapes=[pltpu.VMEM((R, S), F32), pltpu.VMEM((R, LANES), F32)],
        compiler_params=pltpu.CompilerParams(dimension_semantics=("arbitrary", "arbitrary"),
                                             vmem_limit_bytes=VMEM_LIMIT),
        name="topk",
    )(afft, tri)


def _moe_kernel(idx_ref, x1_hbm, gate_ref, wg_ref, wu_ref, wd_ref, ng_ref, out_hbm,
                acc, hbuf, xg, ybuf, sem_in, sem, *, S, cap):
    g = pl.program_id(0)
    e2 = pl.program_id(1)
    k = pl.program_id(2)
    nexp = 2 * pl.num_programs(1)
    half = D_MODEL // 2
    rows0 = g * (MOE_GROUP * S)
    hi_mask = jnp.int32(-65536)

    def lists(seq, ee):
        return ((g * MOE_GROUP + seq) * nexp + ee) * cap

    def gather_jobs(seq, ee, par):
        base = lists(seq, ee)

        def job(c0):
            for c in range(c0, c0 + SCATTER_ROWS):
                r = seq * S + idx_ref[base + c]
                xg[par, seq * cap + c:seq * cap + c + 1, :] = hbuf[pl.ds(r, 1), :]
        return [functools.partial(job, c0) for c0 in range(0, cap, SCATTER_ROWS)]

    def scatter_jobs(seq, ee, par):
        base = lists(seq, ee)

        def job(c0):
            rows = [seq * S + idx_ref[base + c] for c in range(c0, c0 + SCATTER_ROWS)]
            old = [acc[pl.ds(r, 1), :] for r in rows]
            for j, r in enumerate(rows):
                c = seq * cap + c0 + j
                acc[pl.ds(r, 1), :] = old[j] + ybuf[par, c:c + 1, :]
        return [functools.partial(job, c0) for c0 in range(0, cap, SCATTER_ROWS)]

    def ffn(par, accumulate, jobs):
        pieces = 2 * (wg_ref.shape[2] // FF_CHUNK) if jobs else 1
        share = -(-len(jobs) // pieces)
        jobs = list(jobs)

        def run_share():
            for job in jobs[:share]:
                job()
            del jobs[:share]

        p = xg[par]
        lo = pltpu.bitcast(p << 16, F32).astype(BF16)
        hi = pltpu.bitcast(p & hi_mask, F32).astype(BF16)
        x = jnp.concatenate([lo, hi], axis=1)
        hid = []
        for f in range(wg_ref.shape[2] // FF_CHUNK):
            cols = slice(f * FF_CHUNK, (f + 1) * FF_CHUNK)
            gt = jnp.dot(x, wg_ref[0, :, cols], preferred_element_type=F32)
            up = jnp.dot(x, wu_ref[0, :, cols], preferred_element_type=F32)
            hid.append((gt * jax.nn.sigmoid(gt) * up).astype(BF16))
            run_share()
        hid = jnp.concatenate(hid, axis=1)
        gate = jnp.concatenate([gate_ref[s, 0] for s in range(MOE_GROUP)], axis=0)
        for n in range(D_MODEL // FF_CHUNK):
            cols = slice(n * FF_CHUNK, (n + 1) * FF_CHUNK)
            y = jnp.dot(hid, wd_ref[0, :, cols], preferred_element_type=F32) * gate
            ybuf[par, :, cols] = ybuf[par, :, cols] + y if accumulate else y
            run_share()
        assert not jobs

    @pl.when((e2 == 0) & (k == 0))
    def _():
        def load(i):
            start = i * cap if isinstance(i, int) else pl.multiple_of(i * cap, cap)
            return pltpu.make_async_copy(x1_hbm.at[pl.ds(rows0 + start, cap), :],
                                         acc.at[pl.ds(start, cap), :], sem_in.at[i])

        for i in range(sem_in.shape[0]):
            load(i).start()

        def pack(i, _):
            load(i).wait()
            rows = pl.ds(pl.multiple_of(i * cap, cap), cap)
            hb = _rms(acc[rows, :], ng_ref[...]).astype(BF16).astype(F32)
            u = pltpu.bitcast(hb, jnp.int32)
            hbuf[rows, :] = lax.shift_right_logical(u[:, :half], 16) | (u[:, half:] & hi_mask)
            return 0
        lax.fori_loop(0, MOE_GROUP * S // cap, pack, 0)
        ybuf[1] = jnp.zeros(ybuf.shape[1:], F32)
        for seq in range(MOE_GROUP):
            for job in gather_jobs(seq, 0, 0):
                job()

    for kk in range(4):
        @pl.when(k == kk)
        def _(par=kk // 2, fh=kk % 2):
            e = 2 * e2 + par
            sc = scatter_jobs(fh, jnp.maximum(e - 1, 0), 1 - par)
            ga = gather_jobs(fh, jnp.minimum(e + 1, nexp - 1), 1 - par)
            ffn(par, fh == 1, [job for pair in zip(sc, ga) for job in pair])

    @pl.when((e2 == pl.num_programs(1) - 1) & (k == 3))
    def _():
        for seq in range(MOE_GROUP):
            for job in scatter_jobs(seq, nexp - 1, 1):
                job()
        cp = pltpu.make_async_copy(acc, out_hbm.at[pl.ds(rows0, MOE_GROUP * S), :], sem)
        cp.start()
        cp.wait()


def _moe(B, S, cap, idx_flat, x1, gates, wg, wu, wd, ng):
    G = MOE_GROUP
    T = B * S
    fhw = EXPERT_FF // 2
    ex = lambda e2, k: 2 * e2 + k // 2
    fh = lambda k: k % 2
    grid_spec = pltpu.PrefetchScalarGridSpec(
        num_scalar_prefetch=1,
        grid=(B // G, N_EXPERTS // 2, 4),
        in_specs=[pl.BlockSpec(memory_space=pl.ANY),
                  pl.BlockSpec((G, 1, cap, 1), lambda g, e2, k, idx: (g, ex(e2, k), 0, 0)),
                  pl.BlockSpec((1, D_MODEL, fhw), lambda g, e2, k, idx: (ex(e2, k), 0, fh(k))),
                  pl.BlockSpec((1, D_MODEL, fhw), lambda g, e2, k, idx: (ex(e2, k), 0, fh(k))),
                  pl.BlockSpec((1, fhw, D_MODEL), lambda g, e2, k, idx: (ex(e2, k), fh(k), 0)),
                  pl.BlockSpec((1, D_MODEL), lambda g, e2, k, idx: (0, 0))],
        out_specs=pl.BlockSpec(memory_space=pl.ANY),
        scratch_shapes=[pltpu.VMEM((G * S, D_MODEL), F32),
                        pltpu.VMEM((G * S, D_MODEL // 2), jnp.int32),
                        pltpu.VMEM((2, G * cap, D_MODEL // 2), jnp.int32),
                        pltpu.VMEM((2, G * cap, D_MODEL), F32),
                        pltpu.SemaphoreType.DMA((G * S // cap,)),
                        pltpu.SemaphoreType.DMA(())],
    )
    return pl.pallas_call(
        functools.partial(_moe_kernel, S=S, cap=cap),
        grid_spec=grid_spec,
        out_shape=jax.ShapeDtypeStruct((T, D_MODEL), F32),
        input_output_aliases={1: 0},
        compiler_params=pltpu.CompilerParams(dimension_semantics=("arbitrary",) * 3,
                                             vmem_limit_bytes=VMEM_LIMIT),
        name="moe",
    )(idx_flat, x1, gates, wg, wu, wd, ng)


def _t5_bucket(rel):
    nb = N_BUCKETS // 2
    max_exact = nb // 2
    ret = (rel > 0).astype(np.int32) * nb
    n = np.abs(rel)
    large = max_exact + (np.log(np.maximum(n, 1) / max_exact) / np.log(MAX_DISTANCE / max_exact)
                         * (nb - max_exact)).astype(np.int32)
    large = np.minimum(large, nb - 1)
    return (ret + np.where(n < max_exact, n, large)).astype(np.int32)


def _bias_kernel(table_ref, bucket_ref, out_ref):
    bucket = bucket_ref[...]
    row = lax.broadcasted_iota(jnp.int32, (BLOCK, SPAN), 0)
    col = lax.broadcasted_iota(jnp.int32, (BLOCK, SPAN), 1)
    band = jnp.abs(col - WINDOW - row) <= WINDOW
    valid = (band & (col >= WINDOW), band, band & (col < WINDOW + BLOCK))
    for h in range(N_HEADS):
        b = jnp.zeros((BLOCK, SPAN), F32)
        for n in range(N_BUCKETS):
            b = jnp.where(bucket == n, table_ref[n, h], b)
        kh, slab, half = h // Q_GROUP, (h % Q_GROUP) // 2, h % 2
        for var in range(3):
            out_ref[var, kh, slab * BLOCK:(slab + 1) * BLOCK, half * SPAN:(half + 1) * SPAN] = (
                jnp.where(valid[var], b, -jnp.inf))


def _attention_bias(rel_table):
    rel = (np.arange(SPAN)[None, :] - WINDOW) - np.arange(BLOCK)[:, None]
    bucket = jnp.asarray(_t5_bucket(rel))
    return pl.pallas_call(
        _bias_kernel,
        in_specs=[pl.BlockSpec(memory_space=pltpu.SMEM),
                  pl.BlockSpec(bucket.shape, lambda: (0, 0))],
        out_specs=pl.BlockSpec((3, N_KV_HEADS, 2 * BLOCK, 2 * SPAN), lambda: (0, 0, 0, 0)),
        out_shape=jax.ShapeDtypeStruct((3, N_KV_HEADS, 2 * BLOCK, 2 * SPAN), F32),
        name="t5bias",
    )(rel_table.astype(F32), bucket)


def _dup_heads(w):
    parts = []
    for h in range(N_KV_HEADS):
        wh = w[..., h * HEAD_DIM:(h + 1) * HEAD_DIM]
        parts += [wh, wh]
    return jnp.concatenate(parts, axis=-1)


def _mean_matrix(width):
    blk = np.arange(width) // HEAD_DIM
    return jnp.asarray((blk[:, None] == blk[None, :]).astype(np.float32) / HEAD_DIM, dtype=BF16)


def kernel(x, norm_mix_g, w_in, b_gate, vnorm_g, w_spatial, b_spatial, q_norm_g, k_norm_g,
           attn_sink, rel_bias_table, w_proj_a, w_proj_b, w_out, norm_ffn_g, w_router,
           w_gate_e, w_up_e, w_down_e):
    B, S, _ = x.shape
    T = B * S
    cap = CAPACITY_FACTOR * S // N_EXPERTS
    assert S % TQ_MIX == 0 and T % TM_INPROJ == 0 and B % MOE_GROUP == 0
    assert B % min(TOPK_BATCHES, B) == 0
    xf = x.reshape(T, D_MODEL)
    l = 0

    w = w_in[l]
    o_q = 2 * A_WIDTH
    o_k = o_q + B_WIDTH
    o_v = o_k + KV_WIDTH
    o_g = o_v + KV_WIDTH
    w_all = jnp.concatenate([w[:, :o_k], _dup_heads(w[:, o_k:o_v]), _dup_heads(w[:, o_v:o_g]),
                             w[:, o_g:]], axis=1).astype(BF16)
    qg = jnp.tile(q_norm_g[l], N_HEADS)[None, :] * (HEAD_DIM ** -0.5)
    kg = jnp.tile(k_norm_g[l], 2 * N_KV_HEADS)[None, :]
    wsp = w_spatial[l].reshape(A_GROUPS // 2, 2, CHUNK, CHUNK).transpose(0, 2, 1, 3)
    wsp = wsp.reshape(A_GROUPS // 2, CHUNK, 2 * CHUNK).astype(BF16)
    bsp = jnp.repeat(b_spatial[l].T, A_GROUP_DIM, axis=1)
    bias = _attention_bias(rel_bias_table)
    sink = attn_sink[l].astype(F32).reshape(N_KV_HEADS, 2, 2)
    sink = jnp.broadcast_to(sink.transpose(0, 2, 1)[:, :, :, None, None],
                            (N_KV_HEADS, 2, 2, BLOCK, 1)).reshape(N_KV_HEADS, 2, 2 * BLOCK, 1)
    wr = w_router[l]
    wrh = wr.astype(BF16)
    wrl = (wr - wrh.astype(F32)).astype(BF16)
    pad = jnp.zeros((D_MODEL, LANES - N_EXPERTS), BF16)
    wr2 = jnp.concatenate([wrh, pad, wrl, pad], axis=1)
    tri = jnp.asarray(np.triu(np.ones((LANES, LANES), np.float32), k=1), dtype=BF16)

    (u, vn, q, k, v, ga, gb), (wg, wu, wd) = _inproj(
        xf, norm_mix_g[l][None, :], w_all, b_gate[l][None, :], vnorm_g[l][None, :], qg, kg,
        _mean_matrix(B_WIDTH), _mean_matrix(KV_DUP), (w_gate_e[l], w_up_e[l], w_down_e[l]))
    x1, aff = _mix(B, S, u, vn, q, k, v, ga, gb, xf, wsp, bsp, bias, sink,
                   w_proj_a[l].astype(BF16), w_proj_b[l].astype(BF16), w_out[l].astype(BF16),
                   norm_ffn_g[l][None, :], wr2)
    afft = aff.reshape(B, S, N_EXPERTS).transpose(0, 2, 1).reshape(B * N_EXPERTS, S)
    idx, gates = _topk(B, S, cap, afft, tri)
    out = _moe(B, S, cap, idx.reshape(-1), x1, gates.reshape(B, N_EXPERTS, cap, 1), wg, wu, wd,
               norm_ffn_g[l][None, :])
    return out.reshape(B, S, D_MODEL)
```

```python
import functools

import numpy as np
import jax
import jax.numpy as jnp
from jax import lax
from jax.experimental import pallas as pl
from jax.experimental.pallas import tpu as pltpu

F32 = jnp.float32
BF16 = jnp.bfloat16

D_MODEL = 1024
A_GROUPS = 8
A_GROUP_DIM = 64
A_WIDTH = A_GROUPS * A_GROUP_DIM
CHUNK = 128
N_HEADS = 8
N_KV_HEADS = 2
HEAD_DIM = 64
Q_GROUP = N_HEADS // N_KV_HEADS
B_WIDTH = N_HEADS * HEAD_DIM
KV_WIDTH = N_KV_HEADS * HEAD_DIM
WINDOW = 128
BLOCK = 128
SPAN = BLOCK + 2 * WINDOW
N_BUCKETS = 32
MAX_DISTANCE = 128
N_EXPERTS = 16
EXPERT_FF = 2048
CAPACITY_FACTOR = 2
EPS = 1e-6

LANES = 128
KV_DUP = 2 * KV_WIDTH
TM_INPROJ = 512
TQ_MIX = 512
FF_CHUNK = 256
MOE_GROUP = 2
SCATTER_ROWS = 8
TOPK_BATCHES = 8
TOPK_ROWS = 8
VMEM_LIMIT = 56 * 1024 * 1024
INF_BITS = 0x7F800000


def _rms(x, g):
    return x * lax.rsqrt(jnp.mean(x * x, axis=-1, keepdims=True) + EPS) * g


def _inproj_kernel(x_ref, g_ref, w_ref, bg_ref, vg_ref, qg_ref, kg_ref, bdq_ref, bdk_ref,
                   ewg_ref, ewu_ref, ewd_ref,
                   u_ref, vn_ref, q_ref, k_ref, v_ref, ga_ref, gb_ref, bwg_ref, bwu_ref, bwd_ref):
    bwg_ref[...] = ewg_ref[...].astype(BF16)
    bwu_ref[...] = ewu_ref[...].astype(BF16)
    bwd_ref[...] = ewd_ref[...].astype(BF16)

    hb = _rms(x_ref[...], g_ref[...]).astype(BF16)

    def seg(lo, hi):
        return jnp.dot(hb, w_ref[:, lo:hi], preferred_element_type=F32)

    o_va, o_q, o_k = A_WIDTH, 2 * A_WIDTH, 2 * A_WIDTH + B_WIDTH
    o_ga = o_k + 2 * KV_DUP
    o_gb = o_ga + D_MODEL
    ga_ref[...] = jax.nn.sigmoid(seg(o_ga, o_gb) + bg_ref[:, :D_MODEL]).astype(BF16)
    gb_ref[...] = jax.nn.sigmoid(seg(o_gb, o_gb + D_MODEL) + bg_ref[:, D_MODEL:]).astype(BF16)
    u_ref[...] = jax.nn.gelu(seg(0, o_va)).astype(BF16)
    gv = jax.nn.gelu(seg(o_va, o_q))
    vn_ref[...] = _rms(gv, vg_ref[...]).astype(BF16)
    zq = seg(o_q, o_k)
    msq = jnp.dot((zq * zq).astype(BF16), bdq_ref[...], preferred_element_type=F32)
    q_ref[...] = (zq * lax.rsqrt(msq + EPS) * qg_ref[...]).astype(BF16)
    zkv = seg(o_k, o_ga)
    zk = zkv[:, :KV_DUP]
    msk = jnp.dot((zk * zk).astype(BF16), bdk_ref[...], preferred_element_type=F32)
    k_ref[...] = (zk * lax.rsqrt(msk + EPS) * kg_ref[...]).astype(BF16)
    v_ref[...] = zkv[:, KV_DUP:].astype(BF16)


def _inproj(xf, norm_g, w_all, b_gate, vnorm_g, qg, kg, bdq, bdk, expert_w):
    T = xf.shape[0]
    tm = TM_INPROJ
    steps = T // tm
    ncol = w_all.shape[1]
    row = lambda w: pl.BlockSpec((tm, w), lambda i: (i, 0))
    full = lambda a: pl.BlockSpec(a.shape, lambda i: (0,) * a.ndim)
    out_w = (A_WIDTH, A_WIDTH, B_WIDTH, KV_DUP, KV_DUP, D_MODEL, D_MODEL)
    flat = [w.reshape(-1, w.shape[-1]) for w in expert_w]
    assert all(w.shape[0] % (steps * 16) == 0 for w in flat)
    slab = lambda w: pl.BlockSpec((w.shape[0] // steps, w.shape[1]), lambda i: (i, 0))
    outs = pl.pallas_call(
        _inproj_kernel,
        grid=(steps,),
        in_specs=[row(D_MODEL), full(norm_g), pl.BlockSpec((D_MODEL, ncol), lambda i: (0, 0)),
                  full(b_gate), full(vnorm_g), full(qg), full(kg), full(bdq), full(bdk)]
                 + [slab(w) for w in flat],
        out_specs=[row(w) for w in out_w] + [slab(w) for w in flat],
        out_shape=[jax.ShapeDtypeStruct((T, w), BF16) for w in out_w]
                  + [jax.ShapeDtypeStruct(w.shape, BF16) for w in flat],
        compiler_params=pltpu.CompilerParams(dimension_semantics=("arbitrary",),
                                             vmem_limit_bytes=VMEM_LIMIT),
        name="inproj",
    )(xf, norm_g, w_all, b_gate, vnorm_g, qg, kg, bdq, bdk, *flat)
    return outs[:len(out_w)], [o.reshape(w.shape) for o, w in zip(outs[len(out_w):], expert_w)]


def _block_diag_pair(slab, lo_mask):
    zero = jnp.zeros_like(slab)
    return jnp.concatenate([jnp.where(lo_mask, slab, zero), jnp.where(lo_mask, zero, slab)], axis=0)


def _mix_kernel(u_ref, vn_ref, q_ref, kc_ref, kp_ref, kn_ref, vc_ref, vp_ref, vx_ref,
                ga_ref, gb_ref, x_ref, wsp_ref, bsp_ref, bias_ref, sink_ref,
                wpa_ref, wpb_ref, wo_ref, ng_ref, wr_ref,
                x1_ref, aff_ref, kbuf, vbuf, abuf, obuf):
    tq = x_ref.shape[0]
    nblk = tq // BLOCK
    i = pl.program_id(1)
    last = pl.num_programs(1) - 1
    lo_mask = lax.broadcasted_iota(jnp.int32, (1, LANES), 1) < HEAD_DIM

    kbuf[0:WINDOW, :] = kp_ref[...]
    kbuf[WINDOW:WINDOW + tq, :] = kc_ref[...]
    kbuf[WINDOW + tq:, :] = kn_ref[...]
    vbuf[0:WINDOW, :] = vp_ref[...]
    vbuf[WINDOW:WINDOW + tq, :] = vc_ref[...]
    vbuf[WINDOW + tq:, :] = vx_ref[...]

    def spatial_gating(c):
        rows = slice(c * CHUNK, (c + 1) * CHUNK)
        parts = []
        for j in range(A_GROUPS // 2):
            bd = _block_diag_pair(vn_ref[rows, j * LANES:(j + 1) * LANES], lo_mask)
            parts.append(jnp.dot(wsp_ref[j], bd, preferred_element_type=F32))
        mixed = jnp.concatenate(parts, axis=1) + bsp_ref[...]
        abuf[rows, :] = (u_ref[rows, :].astype(F32) * mixed).astype(BF16)

    vrow = lax.broadcasted_iota(jnp.int32, (2 * SPAN, LANES), 0)
    vcol = lax.broadcasted_iota(jnp.int32, (2 * SPAN, LANES), 1)
    ones_cols = jnp.where((vrow < SPAN) == (vcol < HEAD_DIM), 1.0, 0.0).astype(BF16)

    def attention_stages(blocks):
        units = [(n, kh) for n in blocks for kh in range(N_KV_HEADS)]
        st = {}

        def geometry(n, kh):
            if n == 0:
                var = jnp.where(i == 0, 0, 1)
            elif n == nblk - 1:
                var = jnp.where(i == last, 2, 1)
            else:
                var = 1
            return (slice(n * BLOCK, (n + 1) * BLOCK), slice(n * BLOCK, n * BLOCK + SPAN),
                    slice(kh * LANES, (kh + 1) * LANES), var)

        def scores():
            for n, kh in units:
                rows, span, ksl, var = geometry(n, kh)
                kk = _block_diag_pair(kbuf[span, ksl], lo_mask)
                q2 = jnp.concatenate([q_ref[rows, (2 * kh) * LANES:(2 * kh + 1) * LANES],
                                      q_ref[rows, (2 * kh + 1) * LANES:(2 * kh + 2) * LANES]], axis=0)
                st["s", n, kh] = (lax.dot_general(q2, kk, (((1,), (1,)), ((), ())),
                                                  preferred_element_type=F32)
                                  + bias_ref[var, kh])

        def numerators():
            for n, kh in units:
                p2, m2 = [], []
                for half in range(2):
                    sh = st["s", n, kh][:, half * SPAN:(half + 1) * SPAN]
                    m = jnp.maximum(jnp.max(sh, axis=-1, keepdims=True), sink_ref[kh, half])
                    p2.append(jnp.exp(sh - m).astype(BF16))
                    m2.append(m)
                st["p", n, kh] = jnp.concatenate(p2, axis=1)
                st["m", n, kh] = m2

        def values():
            for n, kh in units:
                rows, span, ksl, _ = geometry(n, kh)
                vv = jnp.concatenate([_block_diag_pair(vbuf[span, ksl], lo_mask), ones_cols], axis=1)
                o2 = jnp.dot(st["p", n, kh], vv, preferred_element_type=F32)
                m2 = st["m", n, kh]
                sink_term = jnp.where(lo_mask, jnp.exp(sink_ref[kh, 0] - m2[0]),
                                      jnp.exp(sink_ref[kh, 1] - m2[1]))
                o = o2[:, :LANES] * (1.0 / (o2[:, LANES:] + sink_term))
                obuf[rows, (2 * kh) * LANES:(2 * kh + 1) * LANES] = o[:BLOCK].astype(BF16)
                obuf[rows, (2 * kh + 1) * LANES:(2 * kh + 2) * LANES] = o[BLOCK:].astype(BF16)

        return [scores, numerators, values]

    def projection_pieces(rows):
        st = {}

        def branches():
            pa = jnp.dot(abuf[rows, :], wpa_ref[...], preferred_element_type=F32)
            pb = jnp.dot(obuf[rows, :], wpb_ref[...], preferred_element_type=F32)
            st["merged"] = (ga_ref[rows, :].astype(F32) * pa + gb_ref[rows, :].astype(F32) * pb).astype(BF16)

        def output():
            st["x1"] = x_ref[rows, :] + jnp.dot(st["merged"], wo_ref[...], preferred_element_type=F32)
            x1_ref[rows, :] = st["x1"]

        def router():
            h2 = _rms(st["x1"], ng_ref[...])
            hi = h2.astype(BF16)
            lo = (h2 - hi.astype(F32)).astype(BF16)
            both = jnp.dot(hi, wr_ref[...], preferred_element_type=F32)
            logits = (both[:, :LANES] + jnp.dot(lo, wr_ref[:, :LANES], preferred_element_type=F32)
                      + both[:, LANES:])[:, :N_EXPERTS]
            ex = jnp.exp(logits - jnp.max(logits, axis=-1, keepdims=True))
            aff_ref[rows, :] = ex / jnp.sum(ex, axis=-1, keepdims=True)

        return [branches, output, router]

    for c in range(nblk):
        spatial_gating(c)
    for stage in attention_stages(range(nblk // 2)):
        stage()
    for stage, piece in zip(attention_stages(range(nblk // 2, nblk)),
                            projection_pieces(slice(0, tq // 2))):
        stage()
        piece()
    for piece in projection_pieces(slice(tq // 2, tq)):
        piece()


def _mix(B, S, u, vn, q, k, v, ga, gb, xf, wsp, bsp, bias, sink, wpa, wpb, wo, ng, wr2):
    T = B * S
    tq = TQ_MIX
    nq = S // tq
    bpq = tq // BLOCK
    nb = S // BLOCK
    row = lambda w: pl.BlockSpec((tq, w), lambda b, i: (b * nq + i, 0))
    prev = pl.BlockSpec((BLOCK, KV_DUP), lambda b, i: (b * nb + jnp.maximum(i * bpq - 1, 0), 0))
    nxt = pl.BlockSpec((BLOCK, KV_DUP), lambda b, i: (b * nb + jnp.minimum(i * bpq + bpq, nb - 1), 0))
    full = lambda a: pl.BlockSpec(a.shape, lambda b, i: (0,) * a.ndim)
    return pl.pallas_call(
        _mix_kernel,
        grid=(B, nq),
        in_specs=[row(A_WIDTH), row(A_WIDTH), row(B_WIDTH),
                  row(KV_DUP), prev, nxt, row(KV_DUP), prev, nxt,
                  row(D_MODEL), row(D_MODEL), row(D_MODEL),
                  full(wsp), full(bsp), full(bias), full(sink),
                  full(wpa), full(wpb), full(wo), full(ng), full(wr2)],
        out_specs=[row(D_MODEL), row(N_EXPERTS)],
        out_shape=[jax.ShapeDtypeStruct((T, D_MODEL), F32),
                   jax.ShapeDtypeStruct((T, N_EXPERTS), F32)],
        scratch_shapes=[pltpu.VMEM((tq + 2 * WINDOW, KV_DUP), BF16),
                        pltpu.VMEM((tq + 2 * WINDOW, KV_DUP), BF16),
                        pltpu.VMEM((tq, A_WIDTH), BF16),
                        pltpu.VMEM((tq, B_WIDTH), BF16)],
        compiler_params=pltpu.CompilerParams(dimension_semantics=("arbitrary", "arbitrary"),
                                             vmem_limit_bytes=VMEM_LIMIT),
        name="mix",
    )(u, vn, q, k, k, k, v, v, v, ga, gb, xf, wsp, bsp, bias, sink, wpa, wpb, wo, ng, wr2)


def _topk_kernel(afft_ref, tri_ref, idx_ref, gate_ref, cum, cend, *, cap):
    R, S = afft_ref.shape
    nblk = S // LANES
    j = pl.program_id(1)
    lane = lax.broadcasted_iota(jnp.int32, (1, LANES), 1)

    @pl.when(j == 0)
    def _():
        bits = pltpu.bitcast(afft_ref[...], jnp.int32)

        def search(_, carry):
            lo, hi = carry
            mid = lo + ((hi - lo) >> 1)
            cnt = jnp.sum(jnp.where(bits >= mid, 1.0, 0.0), axis=1, keepdims=True)
            ge = cnt >= cap
            return jnp.where(ge, mid, lo), jnp.where(ge, hi, mid)

        lo0 = jnp.zeros((R, 1), jnp.int32)
        hi0 = jnp.full((R, 1), INF_BITS, jnp.int32)
        thr, _ = lax.fori_loop(0, 31, search, (lo0, hi0))

        def prefix(mask):
            outs, carry, ends = [], jnp.zeros((R, 1), F32), jnp.zeros((R, LANES), F32)
            for jj in range(nblk):
                mj = mask[:, jj * LANES:(jj + 1) * LANES]
                outs.append(jnp.dot(mj.astype(BF16), tri_ref[...], preferred_element_type=F32) + carry)
                carry = carry + jnp.sum(mj, axis=1, keepdims=True)
                ends = jnp.where(lane == jj, carry, ends)
            return jnp.concatenate(outs, axis=1), ends

        gt = jnp.where(bits > thr, 1.0, 0.0)
        eq = jnp.where(bits == thr, 1.0, 0.0)
        need = cap - jnp.sum(gt, axis=1, keepdims=True)
        sel = gt + eq * jnp.where(prefix(eq)[0] < need, 1.0, 0.0)
        before, ends = prefix(sel)
        cum[...] = before + sel
        cend[...] = ends

    slot = lax.broadcasted_iota(jnp.int32, (cap, 1), 0).astype(F32)

    def by_block(row):
        return jnp.concatenate([row[:, b * LANES:(b + 1) * LANES] for b in range(nblk)], axis=0)

    def compact(rr):
        r = j * TOPK_ROWS + rr
        done = jnp.where((cend[pl.ds(r, 1), :] <= slot) & (lane < nblk), 1.0, 0.0)
        blk = jnp.sum(done, axis=1, keepdims=True).astype(jnp.int32)
        pick = jnp.where(lane == blk, 1.0, 0.0).astype(BF16)[:, :nblk]
        cum_b = jnp.dot(pick, by_block(cum[pl.ds(r, 1), :]).astype(BF16), preferred_element_type=F32)
        off = jnp.sum(jnp.where(cum_b <= slot, 1.0, 0.0), axis=1, keepdims=True).astype(jnp.int32)
        idx_ref[rr] = blk * LANES + off
        a = by_block(afft_ref[pl.ds(r, 1), :])
        a_hi = a.astype(BF16)
        r1 = a - a_hi.astype(F32)
        a_mid = r1.astype(BF16)
        a_lo = (r1 - a_mid.astype(F32)).astype(BF16)
        aff_b = (jnp.dot(pick, a_hi, preferred_element_type=F32)
                 + jnp.dot(pick, a_mid, preferred_element_type=F32)
                 + jnp.dot(pick, a_lo, preferred_element_type=F32))
        gate_ref[rr] = jnp.sum(jnp.where(lane == off, aff_b, 0.0), axis=1, keepdims=True)

    for rr in range(TOPK_ROWS):
        compact(rr)


def _topk(B, S, cap, afft, tri):
    nb = min(TOPK_BATCHES, B)
    R = nb * N_EXPERTS
    steps = R // TOPK_ROWS
    out_block = pl.BlockSpec((TOPK_ROWS, cap, 1), lambda i, j: (i * steps + j, 0, 0))
    return pl.pallas_call(
        functools.partial(_topk_kernel, cap=cap),
        grid=(B // nb, steps),
        in_specs=[pl.BlockSpec((R, S), lambda i, j: (i, 0)),
                  pl.BlockSpec(tri.shape, lambda i, j: (0, 0))],
        out_specs=[out_block, out_block],
        out_shape=[jax.ShapeDtypeStruct((B * N_EXPERTS, cap, 1), jnp.int32),
                   jax.ShapeDtypeStruct((B * N_EXPERTS, cap, 1), F32)],
        scratch_shapes=[pltpu.VMEM((R, S), F32), pltpu.VMEM((R, LANES), F32)],
        compiler_params=pltpu.CompilerParams(dimension_semantics=("arbitrary", "arbitrary"),
                                             vmem_limit_bytes=VMEM_LIMIT),
        name="topk",
    )(afft, tri)


def _moe_kernel(idx_ref, x1_hbm, gate_ref, wg_ref, wu_ref, wd_ref, ng_ref, out_hbm,
                acc, hbuf, xg, ybuf, sem_in, sem, *, S, cap):
    g = pl.program_id(0)
    e2 = pl.program_id(1)
    k = pl.program_id(2)
    nexp = 2 * pl.num_programs(1)
    half = D_MODEL // 2
    rows0 = g * (MOE_GROUP * S)
    hi_mask = jnp.int32(-65536)

    def lists(seq, ee):
        return ((g * MOE_GROUP + seq) * nexp + ee) * cap

    def gather_jobs(seq, ee, par):
        base = lists(seq, ee)

        def job(c0):
            for c in range(c0, c0 + SCATTER_ROWS):
                r = seq * S + idx_ref[base + c]
                xg[par, seq * cap + c:seq * cap + c + 1, :] = hbuf[pl.ds(r, 1), :]
        return [functools.partial(job, c0) for c0 in range(0, cap, SCATTER_ROWS)]

    def scatter_jobs(seq, ee, par):
        base = lists(seq, ee)

        def job(c0):
            rows = [seq * S + idx_ref[base + c] for c in range(c0, c0 + SCATTER_ROWS)]
            old = [acc[pl.ds(r, 1), :] for r in rows]
            for j, r in enumerate(rows):
                c = seq * cap + c0 + j
                acc[pl.ds(r, 1), :] = old[j] + ybuf[par, c:c + 1, :]
        return [functools.partial(job, c0) for c0 in range(0, cap, SCATTER_ROWS)]

    def ffn(par, accumulate, jobs):
        pieces = 2 * (wg_ref.shape[2] // FF_CHUNK) if jobs else 1
        share = -(-len(jobs) // pieces)
        jobs = list(jobs)

        def run_share():
            for job in jobs[:share]:
                job()
            del jobs[:share]

        p = xg[par]
        lo = pltpu.bitcast(p << 16, F32).astype(BF16)
        hi = pltpu.bitcast(p & hi_mask, F32).astype(BF16)
        x = jnp.concatenate([lo, hi], axis=1)
        hid = []
        for f in range(wg_ref.shape[2] // FF_CHUNK):
            cols = slice(f * FF_CHUNK, (f + 1) * FF_CHUNK)
            gt = jnp.dot(x, wg_ref[0, :, cols], preferred_element_type=F32)
            up = jnp.dot(x, wu_ref[0, :, cols], preferred_element_type=F32)
            hid.append((gt * jax.nn.sigmoid(gt) * up).astype(BF16))
            run_share()
        hid = jnp.concatenate(hid, axis=1)
        gate = jnp.concatenate([gate_ref[s, 0] for s in range(MOE_GROUP)], axis=0)
        for n in range(D_MODEL // FF_CHUNK):
            cols = slice(n * FF_CHUNK, (n + 1) * FF_CHUNK)
            y = jnp.dot(hid, wd_ref[0, :, cols], preferred_element_type=F32) * gate
            ybuf[par, :, cols] = ybuf[par, :, cols] + y if accumulate else y
            run_share()
        assert not jobs

    @pl.when((e2 == 0) & (k == 0))
    def _():
        def load(i):
            start = i * cap if isinstance(i, int) else pl.multiple_of(i * cap, cap)
            return pltpu.make_async_copy(x1_hbm.at[pl.ds(rows0 + start, cap), :],
                                         acc.at[pl.ds(start, cap), :], sem_in.at[i])

        for i in range(sem_in.shape[0]):
            load(i).start()

        def pack(i, _):
            load(i).wait()
            rows = pl.ds(pl.multiple_of(i * cap, cap), cap)
            hb = _rms(acc[rows, :], ng_ref[...]).astype(BF16).astype(F32)
            u = pltpu.bitcast(hb, jnp.int32)
            hbuf[rows, :] = lax.shift_right_logical(u[:, :half], 16) | (u[:, half:] & hi_mask)
            return 0
        lax.fori_loop(0, MOE_GROUP * S // cap, pack, 0)
        ybuf[1] = jnp.zeros(ybuf.shape[1:], F32)
        for seq in range(MOE_GROUP):
            for job in gather_jobs(seq, 0, 0):
                job()

    for kk in range(4):
        @pl.when(k == kk)
        def _(par=kk // 2, fh=kk % 2):
            e = 2 * e2 + par
            sc = scatter_jobs(fh, jnp.maximum(e - 1, 0), 1 - par)
            ga = gather_jobs(fh, jnp.minimum(e + 1, nexp - 1), 1 - par)
            ffn(par, fh == 1, [job for pair in zip(sc, ga) for job in pair])

    @pl.when((e2 == pl.num_programs(1) - 1) & (k == 3))
    def _():
        for seq in range(MOE_GROUP):
            for job in scatter_jobs(seq, nexp - 1, 1):
                job()
        cp = pltpu.make_async_copy(acc, out_hbm.at[pl.ds(rows0, MOE_GROUP * S), :], sem)
        cp.start()
        cp.wait()


def _moe(B, S, cap, idx_flat, x1, gates, wg, wu, wd, ng):
    G = MOE_GROUP
    T = B * S
    fhw = EXPERT_FF // 2
    ex = lambda e2, k: 2 * e2 + k // 2
    fh = lambda k: k % 2
    grid_spec = pltpu.PrefetchScalarGridSpec(
        num_scalar_prefetch=1,
        grid=(B // G, N_EXPERTS // 2, 4),
        in_specs=[pl.BlockSpec(memory_space=pl.ANY),
                  pl.BlockSpec((G, 1, cap, 1), lambda g, e2, k, idx: (g, ex(e2, k), 0, 0)),
                  pl.BlockSpec((1, D_MODEL, fhw), lambda g, e2, k, idx: (ex(e2, k), 0, fh(k))),
                  pl.BlockSpec((1, D_MODEL, fhw), lambda g, e2, k, idx: (ex(e2, k), 0, fh(k))),
                  pl.BlockSpec((1, fhw, D_MODEL), lambda g, e2, k, idx: (ex(e2, k), fh(k), 0)),
                  pl.BlockSpec((1, D_MODEL), lambda g, e2, k, idx: (0, 0))],
        out_specs=pl.BlockSpec(memory_space=pl.ANY),
        scratch_shapes=[pltpu.VMEM((G * S, D_MODEL), F32),
                        pltpu.VMEM((G * S, D_MODEL // 2), jnp.int32),
                        pltpu.VMEM((2, G * cap, D_MODEL // 2), jnp.int32),
                        pltpu.VMEM((2, G * cap, D_MODEL), F32),
                        pltpu.SemaphoreType.DMA((G * S // cap,)),
                        pltpu.SemaphoreType.DMA(())],
    )
    return pl.pallas_call(
        functools.partial(_moe_kernel, S=S, cap=cap),
        grid_spec=grid_spec,
        out_shape=jax.ShapeDtypeStruct((T, D_MODEL), F32),
        input_output_aliases={1: 0},
        compiler_params=pltpu.CompilerParams(dimension_semantics=("arbitrary",) * 3,
                                             vmem_limit_bytes=VMEM_LIMIT),
        name="moe",
    )(idx_flat, x1, gates, wg, wu, wd, ng)


def _t5_bucket(rel):
    nb = N_BUCKETS // 2
    max_exact = nb // 2
    ret = (rel > 0).astype(np.int32) * nb
    n = np.abs(rel)
    large = max_exact + (np.log(np.maximum(n, 1) / max_exact) / np.log(MAX_DISTANCE / max_exact)
                         * (nb - max_exact)).astype(np.int32)
    large = np.minimum(large, nb - 1)
    return (ret + np.where(n < max_exact, n, large)).astype(np.int32)


def _bias_kernel(table_ref, bucket_ref, out_ref):
    bucket = bucket_ref[...]
    row = lax.broadcasted_iota(jnp.int32, (BLOCK, SPAN), 0)
    col = lax.broadcasted_iota(jnp.int32, (BLOCK, SPAN), 1)
    band = jnp.abs(col - WINDOW - row) <= WINDOW
    valid = (band & (col >= WINDOW), band, band & (col < WINDOW + BLOCK))
    for h in range(N_HEADS):
        b = jnp.zeros((BLOCK, SPAN), F32)
        for n in range(N_BUCKETS):
            b = jnp.where(bucket == n, table_ref[n, h], b)
        kh, slab, half = h // Q_GROUP, (h % Q_GROUP) // 2, h % 2
        for var in range(3):
            out_ref[var, kh, slab * BLOCK:(slab + 1) * BLOCK, half * SPAN:(half + 1) * SPAN] = (
                jnp.where(valid[var], b, -jnp.inf))


def _attention_bias(rel_table):
    rel = (np.arange(SPAN)[None, :] - WINDOW) - np.arange(BLOCK)[:, None]
    bucket = jnp.asarray(_t5_bucket(rel))
    return pl.pallas_call(
        _bias_kernel,
        in_specs=[pl.BlockSpec(memory_space=pltpu.SMEM),
                  pl.BlockSpec(bucket.shape, lambda: (0, 0))],
        out_specs=pl.BlockSpec((3, N_KV_HEADS, 2 * BLOCK, 2 * SPAN), lambda: (0, 0, 0, 0)),
        out_shape=jax.ShapeDtypeStruct((3, N_KV_HEADS, 2 * BLOCK, 2 * SPAN), F32),
        name="t5bias",
    )(rel_table.astype(F32), bucket)


def _dup_heads(w):
    parts = []
    for h in range(N_KV_HEADS):
        wh = w[..., h * HEAD_DIM:(h + 1) * HEAD_DIM]
        parts += [wh, wh]
    return jnp.concatenate(parts, axis=-1)


def _mean_matrix(width):
    blk = np.arange(width) // HEAD_DIM
    return jnp.asarray((blk[:, None] == blk[None, :]).astype(np.float32) / HEAD_DIM, dtype=BF16)


def kernel(x, norm_mix_g, w_in, b_gate, vnorm_g, w_spatial, b_spatial, q_norm_g, k_norm_g,
           attn_sink, rel_bias_table, w_proj_a, w_proj_b, w_out, norm_ffn_g, w_router,
           w_gate_e, w_up_e, w_down_e):
    B, S, _ = x.shape
    T = B * S
    cap = CAPACITY_FACTOR * S // N_EXPERTS
    assert S % TQ_MIX == 0 and T % TM_INPROJ == 0 and B % MOE_GROUP == 0
    assert B % min(TOPK_BATCHES, B) == 0
    xf = x.reshape(T, D_MODEL)
    l = 0

    w = w_in[l]
    o_q = 2 * A_WIDTH
    o_k = o_q + B_WIDTH
    o_v = o_k + KV_WIDTH
    o_g = o_v + KV_WIDTH
    w_all = jnp.concatenate([w[:, :o_k], _dup_heads(w[:, o_k:o_v]), _dup_heads(w[:, o_v:o_g]),
                             w[:, o_g:]], axis=1).astype(BF16)
    qg = jnp.tile(q_norm_g[l], N_HEADS)[None, :] * (HEAD_DIM ** -0.5)
    kg = jnp.tile(k_norm_g[l], 2 * N_KV_HEADS)[None, :]
    wsp = w_spatial[l].reshape(A_GROUPS // 2, 2, CHUNK, CHUNK).transpose(0, 2, 1, 3)
    wsp = wsp.reshape(A_GROUPS // 2, CHUNK, 2 * CHUNK).astype(BF16)
    bsp = jnp.repeat(b_spatial[l].T, A_GROUP_DIM, axis=1)
    bias = _attention_bias(rel_bias_table)
    sink = attn_sink[l].astype(F32).reshape(N_KV_HEADS, 2, 2)
    sink = jnp.broadcast_to(sink.transpose(0, 2, 1)[:, :, :, None, None],
                            (N_KV_HEADS, 2, 2, BLOCK, 1)).reshape(N_KV_HEADS, 2, 2 * BLOCK, 1)
    wr = w_router[l]
    wrh = wr.astype(BF16)
    wrl = (wr - wrh.astype(F32)).astype(BF16)
    pad = jnp.zeros((D_MODEL, LANES - N_EXPERTS), BF16)
    wr2 = jnp.concatenate([wrh, pad, wrl, pad], axis=1)
    tri = jnp.asarray(np.triu(np.ones((LANES, LANES), np.float32), k=1), dtype=BF16)

    (u, vn, q, k, v, ga, gb), (wg, wu, wd) = _inproj(
        xf, norm_mix_g[l][None, :], w_all, b_gate[l][None, :], vnorm_g[l][None, :], qg, kg,
        _mean_matrix(B_WIDTH), _mean_matrix(KV_DUP), (w_gate_e[l], w_up_e[l], w_down_e[l]))
    x1, aff = _mix(B, S, u, vn, q, k, v, ga, gb, xf, wsp, bsp, bias, sink,
                   w_proj_a[l].astype(BF16), w_proj_b[l].astype(BF16), w_out[l].astype(BF16),
                   norm_ffn_g[l][None, :], wr2)
    afft = aff.reshape(B, S, N_EXPERTS).transpose(0, 2, 1).reshape(B * N_EXPERTS, S)
    idx, gates = _topk(B, S, cap, afft, tri)
    out = _moe(B, S, cap, idx.reshape(-1), x1, gates.reshape(B, N_EXPERTS, cap, 1), wg, wu, wd,
               norm_ffn_g[l][None, :])
    return out.reshape(B, S, D_MODEL)
```

```python
import functools

import numpy as np
import jax
import jax.numpy as jnp
from jax import lax
from jax.experimental import pallas as pl
from jax.experimental.pallas import tpu as pltpu

F32 = jnp.float32
BF16 = jnp.bfloat16

D_MODEL = 1024
A_GROUPS = 8
A_GROUP_DIM = 64
A_WIDTH = A_GROUPS * A_GROUP_DIM
CHUNK = 128
N_HEADS = 8
N_KV_HEADS = 2
HEAD_DIM = 64
Q_GROUP = N_HEADS // N_KV_HEADS
B_WIDTH = N_HEADS * HEAD_DIM
KV_WIDTH = N_KV_HEADS * HEAD_DIM
WINDOW = 128
BLOCK = 128
SPAN = BLOCK + 2 * WINDOW
N_BUCKETS = 32
MAX_DISTANCE = 128
N_EXPERTS = 16
EXPERT_FF = 2048
CAPACITY_FACTOR = 2
EPS = 1e-6

LANES = 128
KV_DUP = 2 * KV_WIDTH
TM_INPROJ = 512
TQ_MIX = 512
FF_CHUNK = 256
MOE_GROUP = 2
SCATTER_ROWS = 8
STAGE_SLOTS = 4
TOPK_BATCHES = 8
TOPK_ROWS = 8
VMEM_LIMIT = 56 * 1024 * 1024
INF_BITS = 0x7F800000


def _rms(x, g):
    return x * lax.rsqrt(jnp.mean(x * x, axis=-1, keepdims=True) + EPS) * g


def _inproj_kernel(x_ref, g_ref, w_ref, bg_ref, vg_ref, qg_ref, kg_ref, bdq_ref, bdk_ref,
                   ewg_ref, ewu_ref, ewd_ref,
                   u_ref, vn_ref, q_ref, k_ref, v_ref, ga_ref, gb_ref, bwg_ref, bwu_ref, bwd_ref):
    bwg_ref[...] = ewg_ref[...].astype(BF16)
    bwu_ref[...] = ewu_ref[...].astype(BF16)
    bwd_ref[...] = ewd_ref[...].astype(BF16)

    hb = _rms(x_ref[...], g_ref[...]).astype(BF16)

    def seg(lo, hi):
        return jnp.dot(hb, w_ref[:, lo:hi], preferred_element_type=F32)

    o_va, o_q, o_k = A_WIDTH, 2 * A_WIDTH, 2 * A_WIDTH + B_WIDTH
    o_ga = o_k + 2 * KV_DUP
    o_gb = o_ga + D_MODEL
    ga_ref[...] = jax.nn.sigmoid(seg(o_ga, o_gb) + bg_ref[:, :D_MODEL]).astype(BF16)
    gb_ref[...] = jax.nn.sigmoid(seg(o_gb, o_gb + D_MODEL) + bg_ref[:, D_MODEL:]).astype(BF16)
    u_ref[...] = jax.nn.gelu(seg(0, o_va)).astype(BF16)
    gv = jax.nn.gelu(seg(o_va, o_q))
    vn_ref[...] = _rms(gv, vg_ref[...]).astype(BF16)
    zq = seg(o_q, o_k)
    msq = jnp.dot((zq * zq).astype(BF16), bdq_ref[...], preferred_element_type=F32)
    q_ref[...] = (zq * lax.rsqrt(msq + EPS) * qg_ref[...]).astype(BF16)
    zkv = seg(o_k, o_ga)
    zk = zkv[:, :KV_DUP]
    msk = jnp.dot((zk * zk).astype(BF16), bdk_ref[...], preferred_element_type=F32)
    k_ref[...] = (zk * lax.rsqrt(msk + EPS) * kg_ref[...]).astype(BF16)
    v_ref[...] = zkv[:, KV_DUP:].astype(BF16)


def _inproj(xf, norm_g, w_all, b_gate, vnorm_g, qg, kg, bdq, bdk, expert_w):
    T = xf.shape[0]
    tm = TM_INPROJ
    steps = T // tm
    ncol = w_all.shape[1]
    row = lambda w: pl.BlockSpec((tm, w), lambda i: (i, 0))
    full = lambda a: pl.BlockSpec(a.shape, lambda i: (0,) * a.ndim)
    out_w = (A_WIDTH, A_WIDTH, B_WIDTH, KV_DUP, KV_DUP, D_MODEL, D_MODEL)
    flat = [w.reshape(-1, w.shape[-1]) for w in expert_w]
    assert all(w.shape[0] % (steps * 16) == 0 for w in flat)
    slab = lambda w: pl.BlockSpec((w.shape[0] // steps, w.shape[1]), lambda i: (i, 0))
    outs = pl.pallas_call(
        _inproj_kernel,
        grid=(steps,),
        in_specs=[row(D_MODEL), full(norm_g), pl.BlockSpec((D_MODEL, ncol), lambda i: (0, 0)),
                  full(b_gate), full(vnorm_g), full(qg), full(kg), full(bdq), full(bdk)]
                 + [slab(w) for w in flat],
        out_specs=[row(w) for w in out_w] + [slab(w) for w in flat],
        out_shape=[jax.ShapeDtypeStruct((T, w), BF16) for w in out_w]
                  + [jax.ShapeDtypeStruct(w.shape, BF16) for w in flat],
        compiler_params=pltpu.CompilerParams(dimension_semantics=("arbitrary",),
                                             vmem_limit_bytes=VMEM_LIMIT),
        name="inproj",
    )(xf, norm_g, w_all, b_gate, vnorm_g, qg, kg, bdq, bdk, *flat)
    return outs[:len(out_w)], [o.reshape(w.shape) for o, w in zip(outs[len(out_w):], expert_w)]


def _block_diag_pair(slab, lo_mask):
    zero = jnp.zeros_like(slab)
    return jnp.concatenate([jnp.where(lo_mask, slab, zero), jnp.where(lo_mask, zero, slab)], axis=0)


def _mix_kernel(u_ref, vn_ref, q_ref, kc_ref, kp_ref, kn_ref, vc_ref, vp_ref, vx_ref,
                ga_ref, gb_ref, x_ref, wsp_ref, bsp_ref, bias_ref, sink_ref,
                wpa_ref, wpb_ref, wo_ref, ng_ref, wr_ref,
                x1_ref, aff_ref, kbuf, vbuf, abuf, obuf):
    tq = x_ref.shape[0]
    nblk = tq // BLOCK
    i = pl.program_id(1)
    last = pl.num_programs(1) - 1
    lo_mask = lax.broadcasted_iota(jnp.int32, (1, LANES), 1) < HEAD_DIM

    kbuf[0:WINDOW, :] = kp_ref[...]
    kbuf[WINDOW:WINDOW + tq, :] = kc_ref[...]
    kbuf[WINDOW + tq:, :] = kn_ref[...]
    vbuf[0:WINDOW, :] = vp_ref[...]
    vbuf[WINDOW:WINDOW + tq, :] = vc_ref[...]
    vbuf[WINDOW + tq:, :] = vx_ref[...]

    def spatial_gating(c):
        rows = slice(c * CHUNK, (c + 1) * CHUNK)
        parts = []
        for j in range(A_GROUPS // 2):
            bd = _block_diag_pair(vn_ref[rows, j * LANES:(j + 1) * LANES], lo_mask)
            parts.append(jnp.dot(wsp_ref[j], bd, preferred_element_type=F32))
        mixed = jnp.concatenate(parts, axis=1) + bsp_ref[...]
        abuf[rows, :] = (u_ref[rows, :].astype(F32) * mixed).astype(BF16)

    vrow = lax.broadcasted_iota(jnp.int32, (2 * SPAN, LANES), 0)
    vcol = lax.broadcasted_iota(jnp.int32, (2 * SPAN, LANES), 1)
    ones_cols = jnp.where((vrow < SPAN) == (vcol < HEAD_DIM), 1.0, 0.0).astype(BF16)

    def attention_stages(blocks):
        units = [(n, kh) for n in blocks for kh in range(N_KV_HEADS)]
        st = {}

        def geometry(n, kh):
            if n == 0:
                var = jnp.where(i == 0, 0, 1)
            elif n == nblk - 1:
                var = jnp.where(i == last, 2, 1)
            else:
                var = 1
            return (slice(n * BLOCK, (n + 1) * BLOCK), slice(n * BLOCK, n * BLOCK + SPAN),
                    slice(kh * LANES, (kh + 1) * LANES), var)

        def scores():
            for n, kh in units:
                rows, span, ksl, var = geometry(n, kh)
                kk = _block_diag_pair(kbuf[span, ksl], lo_mask)
                q2 = jnp.concatenate([q_ref[rows, (2 * kh) * LANES:(2 * kh + 1) * LANES],
                                      q_ref[rows, (2 * kh + 1) * LANES:(2 * kh + 2) * LANES]], axis=0)
                st["s", n, kh] = (lax.dot_general(q2, kk, (((1,), (1,)), ((), ())),
                                                  preferred_element_type=F32)
                                  + bias_ref[var, kh])

        def numerators():
            for n, kh in units:
                p2, m2 = [], []
                for half in range(2):
                    sh = st["s", n, kh][:, half * SPAN:(half + 1) * SPAN]
                    m = jnp.maximum(jnp.max(sh, axis=-1, keepdims=True), sink_ref[kh, half])
                    p2.append(jnp.exp(sh - m).astype(BF16))
                    m2.append(m)
                st["p", n, kh] = jnp.concatenate(p2, axis=1)
                st["m", n, kh] = m2

        def values():
            for n, kh in units:
                rows, span, ksl, _ = geometry(n, kh)
                vv = jnp.concatenate([_block_diag_pair(vbuf[span, ksl], lo_mask), ones_cols], axis=1)
                o2 = jnp.dot(st["p", n, kh], vv, preferred_element_type=F32)
                m2 = st["m", n, kh]
                sink_term = jnp.where(lo_mask, jnp.exp(sink_ref[kh, 0] - m2[0]),
                                      jnp.exp(sink_ref[kh, 1] - m2[1]))
                o = o2[:, :LANES] * (1.0 / (o2[:, LANES:] + sink_term))
                obuf[rows, (2 * kh) * LANES:(2 * kh + 1) * LANES] = o[:BLOCK].astype(BF16)
                obuf[rows, (2 * kh + 1) * LANES:(2 * kh + 2) * LANES] = o[BLOCK:].astype(BF16)

        return [scores, numerators, values]

    def projection_pieces(rows):
        st = {}

        def branches():
            pa = jnp.dot(abuf[rows, :], wpa_ref[...], preferred_element_type=F32)
            pb = jnp.dot(obuf[rows, :], wpb_ref[...], preferred_element_type=F32)
            st["merged"] = (ga_ref[rows, :].astype(F32) * pa + gb_ref[rows, :].astype(F32) * pb).astype(BF16)

        def output():
            st["x1"] = x_ref[rows, :] + jnp.dot(st["merged"], wo_ref[...], preferred_element_type=F32)
            x1_ref[rows, :] = st["x1"]

        def router():
            h2 = _rms(st["x1"], ng_ref[...])
            hi = h2.astype(BF16)
            lo = (h2 - hi.astype(F32)).astype(BF16)
            both = jnp.dot(hi, wr_ref[...], preferred_element_type=F32)
            logits = (both[:, :LANES] + jnp.dot(lo, wr_ref[:, :LANES], preferred_element_type=F32)
                      + both[:, LANES:])[:, :N_EXPERTS]
            ex = jnp.exp(logits - jnp.max(logits, axis=-1, keepdims=True))
            aff_ref[rows, :] = ex / jnp.sum(ex, axis=-1, keepdims=True)

        return [branches, output, router]

    for c in range(nblk):
        spatial_gating(c)
    for stage in attention_stages(range(nblk // 2)):
        stage()
    for stage, piece in zip(attention_stages(range(nblk // 2, nblk)),
                            projection_pieces(slice(0, tq // 2))):
        stage()
        piece()
    for piece in projection_pieces(slice(tq // 2, tq)):
        piece()


def _mix(B, S, u, vn, q, k, v, ga, gb, xf, wsp, bsp, bias, sink, wpa, wpb, wo, ng, wr2):
    T = B * S
    tq = TQ_MIX
    nq = S // tq
    bpq = tq // BLOCK
    nb = S // BLOCK
    row = lambda w: pl.BlockSpec((tq, w), lambda b, i: (b * nq + i, 0))
    prev = pl.BlockSpec((BLOCK, KV_DUP), lambda b, i: (b * nb + jnp.maximum(i * bpq - 1, 0), 0))
    nxt = pl.BlockSpec((BLOCK, KV_DUP), lambda b, i: (b * nb + jnp.minimum(i * bpq + bpq, nb - 1), 0))
    full = lambda a: pl.BlockSpec(a.shape, lambda b, i: (0,) * a.ndim)
    return pl.pallas_call(
        _mix_kernel,
        grid=(B, nq),
        in_specs=[row(A_WIDTH), row(A_WIDTH), row(B_WIDTH),
                  row(KV_DUP), prev, nxt, row(KV_DUP), prev, nxt,
                  row(D_MODEL), row(D_MODEL), row(D_MODEL),
                  full(wsp), full(bsp), full(bias), full(sink),
                  full(wpa), full(wpb), full(wo), full(ng), full(wr2)],
        out_specs=[row(D_MODEL), row(N_EXPERTS)],
        out_shape=[jax.ShapeDtypeStruct((T, D_MODEL), F32),
                   jax.ShapeDtypeStruct((T, N_EXPERTS), F32)],
        scratch_shapes=[pltpu.VMEM((tq + 2 * WINDOW, KV_DUP), BF16),
                        pltpu.VMEM((tq + 2 * WINDOW, KV_DUP), BF16),
                        pltpu.VMEM((tq, A_WIDTH), BF16),
                        pltpu.VMEM((tq, B_WIDTH), BF16)],
        compiler_params=pltpu.CompilerParams(dimension_semantics=("arbitrary", "arbitrary"),
                                             vmem_limit_bytes=VMEM_LIMIT),
        name="mix",
    )(u, vn, q, k, k, k, v, v, v, ga, gb, xf, wsp, bsp, bias, sink, wpa, wpb, wo, ng, wr2)


def _topk_kernel(afft_ref, tri_ref, idx_ref, gate_ref, cum, cend, *, cap):
    R, S = afft_ref.shape
    nblk = S // LANES
    j = pl.program_id(1)
    lane = lax.broadcasted_iota(jnp.int32, (1, LANES), 1)

    @pl.when(j == 0)
    def _():
        bits = pltpu.bitcast(afft_ref[...], jnp.int32)

        def search(_, carry):
            lo, hi = carry
            mid = lo + ((hi - lo) >> 1)
            cnt = jnp.sum(jnp.where(bits >= mid, 1.0, 0.0), axis=1, keepdims=True)
            ge = cnt >= cap
            return jnp.where(ge, mid, lo), jnp.where(ge, hi, mid)

        lo0 = jnp.zeros((R, 1), jnp.int32)
        hi0 = jnp.full((R, 1), INF_BITS, jnp.int32)
        thr, _ = lax.fori_loop(0, 31, search, (lo0, hi0))

        def prefix(mask):
            outs, carry, ends = [], jnp.zeros((R, 1), F32), jnp.zeros((R, LANES), F32)
            for jj in range(nblk):
                mj = mask[:, jj * LANES:(jj + 1) * LANES]
                outs.append(jnp.dot(mj.astype(BF16), tri_ref[...], preferred_element_type=F32) + carry)
                carry = carry + jnp.sum(mj, axis=1, keepdims=True)
                ends = jnp.where(lane == jj, carry, ends)
            return jnp.concatenate(outs, axis=1), ends

        gt = jnp.where(bits > thr, 1.0, 0.0)
        eq = jnp.where(bits == thr, 1.0, 0.0)
        need = cap - jnp.sum(gt, axis=1, keepdims=True)
        sel = gt + eq * jnp.where(prefix(eq)[0] < need, 1.0, 0.0)
        before, ends = prefix(sel)
        cum[...] = before + sel
        cend[...] = ends

    slot = lax.broadcasted_iota(jnp.int32, (cap, 1), 0).astype(F32)

    def by_block(row):
        return jnp.concatenate([row[:, b * LANES:(b + 1) * LANES] for b in range(nblk)], axis=0)

    def compact(rr):
        r = j * TOPK_ROWS + rr
        done = jnp.where((cend[pl.ds(r, 1), :] <= slot) & (lane < nblk), 1.0, 0.0)
        blk = jnp.sum(done, axis=1, keepdims=True).astype(jnp.int32)
        pick = jnp.where(lane == blk, 1.0, 0.0).astype(BF16)[:, :nblk]
        cum_b = jnp.dot(pick, by_block(cum[pl.ds(r, 1), :]).astype(BF16), preferred_element_type=F32)
        off = jnp.sum(jnp.where(cum_b <= slot, 1.0, 0.0), axis=1, keepdims=True).astype(jnp.int32)
        idx_ref[rr] = blk * LANES + off
        a = by_block(afft_ref[pl.ds(r, 1), :])
        a_hi = a.astype(BF16)
        r1 = a - a_hi.astype(F32)
        a_mid = r1.astype(BF16)
        a_lo = (r1 - a_mid.astype(F32)).astype(BF16)
        aff_b = (jnp.dot(pick, a_hi, preferred_element_type=F32)
                 + jnp.dot(pick, a_mid, preferred_element_type=F32)
                 + jnp.dot(pick, a_lo, preferred_element_type=F32))
        gate_ref[rr] = jnp.sum(jnp.where(lane == off, aff_b, 0.0), axis=1, keepdims=True)

    for rr in range(TOPK_ROWS):
        compact(rr)


def _topk(B, S, cap, afft, tri):
    nb = min(TOPK_BATCHES, B)
    R = nb * N_EXPERTS
    steps = R // TOPK_ROWS
    out_block = pl.BlockSpec((TOPK_ROWS, cap, 1), lambda i, j: (i * steps + j, 0, 0))
    return pl.pallas_call(
        functools.partial(_topk_kernel, cap=cap),
        grid=(B // nb, steps),
        in_specs=[pl.BlockSpec((R, S), lambda i, j: (i, 0)),
                  pl.BlockSpec(tri.shape, lambda i, j: (0, 0))],
        out_specs=[out_block, out_block],
        out_shape=[jax.ShapeDtypeStruct((B * N_EXPERTS, cap, 1), jnp.int32),
                   jax.ShapeDtypeStruct((B * N_EXPERTS, cap, 1), F32)],
        scratch_shapes=[pltpu.VMEM((R, S), F32), pltpu.VMEM((R, LANES), F32)],
        compiler_params=pltpu.CompilerParams(dimension_semantics=("arbitrary", "arbitrary"),
                                             vmem_limit_bytes=VMEM_LIMIT),
        name="topk",
    )(afft, tri)


def _moe_kernel(idx_ref, x1_hbm, gate_ref, wg_ref, wu_ref, wd_ref, ng_ref, out_hbm,
                acc, hbuf, xg, ybuf, stage, sem_in, sem_stage, sem, *, S, cap):
    g = pl.program_id(0)
    e2 = pl.program_id(1)
    k = pl.program_id(2)
    nexp = 2 * pl.num_programs(1)
    half = D_MODEL // 2
    group_rows = MOE_GROUP * S
    rows0 = g * group_rows
    hi_mask = jnp.int32(-65536)

    def lists(seq, ee):
        return ((g * MOE_GROUP + seq) * nexp + ee) * cap

    def gather_jobs(seq, ee, par):
        base = lists(seq, ee)

        def job(c0):
            for c in range(c0, c0 + SCATTER_ROWS):
                r = seq * S + idx_ref[base + c]
                xg[par, seq * cap + c:seq * cap + c + 1, :] = hbuf[pl.ds(r, 1), :]
        return [functools.partial(job, c0) for c0 in range(0, cap, SCATTER_ROWS)]

    def scatter_jobs(seq, ee, par, to_spare=None):
        base = lists(seq, ee)

        def job(c0):
            rows = [seq * S + idx_ref[base + c] for c in range(c0, c0 + SCATTER_ROWS)]
            if to_spare is not None:
                rows = [jnp.where(to_spare, group_rows, r) for r in rows]
            old = [acc[pl.ds(r, 1), :] for r in rows]
            for j, r in enumerate(rows):
                c = seq * cap + c0 + j
                acc[pl.ds(r, 1), :] = old[j] + ybuf[par, c:c + 1, :]
        return [functools.partial(job, c0) for c0 in range(0, cap, SCATTER_ROWS)]

    def ffn(par, accumulate, jobs):
        pieces = 2 * (wg_ref.shape[2] // FF_CHUNK) if jobs else 1
        share = -(-len(jobs) // pieces)
        jobs = list(jobs)

        def run_share():
            for job in jobs[:share]:
                job()
            del jobs[:share]

        p = xg[par]
        lo = pltpu.bitcast(p << 16, F32).astype(BF16)
        hi = pltpu.bitcast(p & hi_mask, F32).astype(BF16)
        x = jnp.concatenate([lo, hi], axis=1)
        hid = []
        for f in range(wg_ref.shape[2] // FF_CHUNK):
            cols = slice(f * FF_CHUNK, (f + 1) * FF_CHUNK)
            gt = jnp.dot(x, wg_ref[0, :, cols], preferred_element_type=F32)
            up = jnp.dot(x, wu_ref[0, :, cols], preferred_element_type=F32)
            hid.append((gt * jax.nn.sigmoid(gt) * up).astype(BF16))
            run_share()
        hid = jnp.concatenate(hid, axis=1)
        gate = jnp.concatenate([gate_ref[s, 0] for s in range(MOE_GROUP)], axis=0)
        for n in range(D_MODEL // FF_CHUNK):
            cols = slice(n * FF_CHUNK, (n + 1) * FF_CHUNK)
            y = jnp.dot(hid, wd_ref[0, :, cols], preferred_element_type=F32) * gate
            ybuf[par, :, cols] = ybuf[par, :, cols] + y if accumulate else y
            run_share()
        assert not jobs

    n_chunks = sem_in.shape[0]

    def chunk_rows(i):
        return pl.ds(i * cap if isinstance(i, int) else pl.multiple_of(i * cap, cap), cap)

    def acc_load(i):
        return pltpu.make_async_copy(x1_hbm.at[pl.ds(rows0 + i * cap, cap), :],
                                     acc.at[chunk_rows(i), :], sem_in.at[i])

    def writeback():
        return pltpu.make_async_copy(acc.at[pl.ds(0, group_rows), :],
                                     out_hbm.at[pl.ds(rows0, group_rows), :], sem)

    @pl.when((e2 == 0) & (k == 0))
    def _():
        def stage_load(i, slot):
            return pltpu.make_async_copy(x1_hbm.at[pl.ds(rows0 + i * cap, cap), :], stage.at[slot],
                                         sem_stage.at[slot])

        depth = stage.shape[0]
        for i in range(depth - 1):
            stage_load(i, i).start()

        def pack(i, _):
            slot = i % depth
            ahead = i + depth - 1

            @pl.when(ahead < n_chunks)
            def _():
                stage_load(ahead, ahead % depth).start()
            stage_load(i, slot).wait()
            hb = _rms(stage[slot], ng_ref[...]).astype(BF16).astype(F32)
            u = pltpu.bitcast(hb, jnp.int32)
            hbuf[chunk_rows(i), :] = lax.shift_right_logical(u[:, :half], 16) | (u[:, half:] & hi_mask)
            return 0
        lax.fori_loop(0, n_chunks, pack, 0)
        ybuf[1] = jnp.zeros(ybuf.shape[1:], F32)
        acc[group_rows:, :] = jnp.zeros((acc.shape[0] - group_rows, D_MODEL), F32)
        for seq in range(MOE_GROUP):
            for job in gather_jobs(seq, 0, 0):
                job()

        @pl.when(g > 0)
        def _():
            writeback().wait()
        for i in range(n_chunks):
            acc_load(i).start()

    @pl.when((e2 == 0) & (k == 2))
    def _():
        for i in range(n_chunks):
            acc_load(i).wait()

    for kk in range(4):
        @pl.when(k == kk)
        def _(par=kk // 2, fh=kk % 2):
            e = 2 * e2 + par
            sc = scatter_jobs(fh, jnp.maximum(e - 1, 0), 1 - par, to_spare=(e == 0) if par == 0 else None)
            ga = gather_jobs(fh, jnp.minimum(e + 1, nexp - 1), 1 - par)
            ffn(par, fh == 1, [job for pair in zip(sc, ga) for job in pair])

    @pl.when((e2 == pl.num_programs(1) - 1) & (k == 3))
    def _():
        for seq in range(MOE_GROUP):
            for job in scatter_jobs(seq, nexp - 1, 1):
                job()
        writeback().start()

        @pl.when(g == pl.num_programs(0) - 1)
        def _():
            writeback().wait()


def _moe(B, S, cap, idx_flat, x1, gates, wg, wu, wd, ng):
    G = MOE_GROUP
    T = B * S
    fhw = EXPERT_FF // 2
    ex = lambda e2, k: 2 * e2 + k // 2
    fh = lambda k: k % 2
    grid_spec = pltpu.PrefetchScalarGridSpec(
        num_scalar_prefetch=1,
        grid=(B // G, N_EXPERTS // 2, 4),
        in_specs=[pl.BlockSpec(memory_space=pl.ANY),
                  pl.BlockSpec((G, 1, cap, 1), lambda g, e2, k, idx: (g, ex(e2, k), 0, 0)),
                  pl.BlockSpec((1, D_MODEL, fhw), lambda g, e2, k, idx: (ex(e2, k), 0, fh(k))),
                  pl.BlockSpec((1, D_MODEL, fhw), lambda g, e2, k, idx: (ex(e2, k), 0, fh(k))),
                  pl.BlockSpec((1, fhw, D_MODEL), lambda g, e2, k, idx: (ex(e2, k), fh(k), 0)),
                  pl.BlockSpec((1, D_MODEL), lambda g, e2, k, idx: (0, 0))],
        out_specs=pl.BlockSpec(memory_space=pl.ANY),
        scratch_shapes=[pltpu.VMEM((G * S + SCATTER_ROWS, D_MODEL), F32),
                        pltpu.VMEM((G * S, D_MODEL // 2), jnp.int32),
                        pltpu.VMEM((2, G * cap, D_MODEL // 2), jnp.int32),
                        pltpu.VMEM((2, G * cap, D_MODEL), F32),
                        pltpu.VMEM((STAGE_SLOTS, cap, D_MODEL), F32),
                        pltpu.SemaphoreType.DMA((G * S // cap,)),
                        pltpu.SemaphoreType.DMA((STAGE_SLOTS,)),
                        pltpu.SemaphoreType.DMA(())],
    )
    return pl.pallas_call(
        functools.partial(_moe_kernel, S=S, cap=cap),
        grid_spec=grid_spec,
        out_shape=jax.ShapeDtypeStruct((T, D_MODEL), F32),
        input_output_aliases={1: 0},
        compiler_params=pltpu.CompilerParams(dimension_semantics=("arbitrary",) * 3,
                                             vmem_limit_bytes=VMEM_LIMIT),
        name="moe",
    )(idx_flat, x1, gates, wg, wu, wd, ng)


def _t5_bucket(rel):
    nb = N_BUCKETS // 2
    max_exact = nb // 2
    ret = (rel > 0).astype(np.int32) * nb
    n = np.abs(rel)
    large = max_exact + (np.log(np.maximum(n, 1) / max_exact) / np.log(MAX_DISTANCE / max_exact)
                         * (nb - max_exact)).astype(np.int32)
    large = np.minimum(large, nb - 1)
    return (ret + np.where(n < max_exact, n, large)).astype(np.int32)


def _bias_kernel(table_ref, bucket_ref, out_ref):
    bucket = bucket_ref[...]
    row = lax.broadcasted_iota(jnp.int32, (BLOCK, SPAN), 0)
    col = lax.broadcasted_iota(jnp.int32, (BLOCK, SPAN), 1)
    band = jnp.abs(col - WINDOW - row) <= WINDOW
    valid = (band & (col >= WINDOW), band, band & (col < WINDOW + BLOCK))
    for h in range(N_HEADS):
        b = jnp.zeros((BLOCK, SPAN), F32)
        for n in range(N_BUCKETS):
            b = jnp.where(bucket == n, table_ref[n, h], b)
        kh, slab, half = h // Q_GROUP, (h % Q_GROUP) // 2, h % 2
        for var in range(3):
            out_ref[var, kh, slab * BLOCK:(slab + 1) * BLOCK, half * SPAN:(half + 1) * SPAN] = (
                jnp.where(valid[var], b, -jnp.inf))


def _attention_bias(rel_table):
    rel = (np.arange(SPAN)[None, :] - WINDOW) - np.arange(BLOCK)[:, None]
    bucket = jnp.asarray(_t5_bucket(rel))
    return pl.pallas_call(
        _bias_kernel,
        in_specs=[pl.BlockSpec(memory_space=pltpu.SMEM),
                  pl.BlockSpec(bucket.shape, lambda: (0, 0))],
        out_specs=pl.BlockSpec((3, N_KV_HEADS, 2 * BLOCK, 2 * SPAN), lambda: (0, 0, 0, 0)),
        out_shape=jax.ShapeDtypeStruct((3, N_KV_HEADS, 2 * BLOCK, 2 * SPAN), F32),
        name="t5bias",
    )(rel_table.astype(F32), bucket)


def _dup_heads(w):
    parts = []
    for h in range(N_KV_HEADS):
        wh = w[..., h * HEAD_DIM:(h + 1) * HEAD_DIM]
        parts += [wh, wh]
    return jnp.concatenate(parts, axis=-1)


def _mean_matrix(width):
    blk = np.arange(width) // HEAD_DIM
    return jnp.asarray((blk[:, None] == blk[None, :]).astype(np.float32) / HEAD_DIM, dtype=BF16)


def kernel(x, norm_mix_g, w_in, b_gate, vnorm_g, w_spatial, b_spatial, q_norm_g, k_norm_g,
           attn_sink, rel_bias_table, w_proj_a, w_proj_b, w_out, norm_ffn_g, w_router,
           w_gate_e, w_up_e, w_down_e):
    B, S, _ = x.shape
    T = B * S
    cap = CAPACITY_FACTOR * S // N_EXPERTS
    assert S % TQ_MIX == 0 and T % TM_INPROJ == 0 and B % MOE_GROUP == 0
    assert B % min(TOPK_BATCHES, B) == 0
    xf = x.reshape(T, D_MODEL)
    l = 0

    w = w_in[l]
    o_q = 2 * A_WIDTH
    o_k = o_q + B_WIDTH
    o_v = o_k + KV_WIDTH
    o_g = o_v + KV_WIDTH
    w_all = jnp.concatenate([w[:, :o_k], _dup_heads(w[:, o_k:o_v]), _dup_heads(w[:, o_v:o_g]),
                             w[:, o_g:]], axis=1).astype(BF16)
    qg = jnp.tile(q_norm_g[l], N_HEADS)[None, :] * (HEAD_DIM ** -0.5)
    kg = jnp.tile(k_norm_g[l], 2 * N_KV_HEADS)[None, :]
    wsp = w_spatial[l].reshape(A_GROUPS // 2, 2, CHUNK, CHUNK).transpose(0, 2, 1, 3)
    wsp = wsp.reshape(A_GROUPS // 2, CHUNK, 2 * CHUNK).astype(BF16)
    bsp = jnp.repeat(b_spatial[l].T, A_GROUP_DIM, axis=1)
    bias = _attention_bias(rel_bias_table)
    sink = attn_sink[l].astype(F32).reshape(N_KV_HEADS, 2, 2)
    sink = jnp.broadcast_to(sink.transpose(0, 2, 1)[:, :, :, None, None],
                            (N_KV_HEADS, 2, 2, BLOCK, 1)).reshape(N_KV_HEADS, 2, 2 * BLOCK, 1)
    wr = w_router[l]
    wrh = wr.astype(BF16)
    wrl = (wr - wrh.astype(F32)).astype(BF16)
    pad = jnp.zeros((D_MODEL, LANES - N_EXPERTS), BF16)
    wr2 = jnp.concatenate([wrh, pad, wrl, pad], axis=1)
    tri = jnp.asarray(np.triu(np.ones((LANES, LANES), np.float32), k=1), dtype=BF16)

    (u, vn, q, k, v, ga, gb), (wg, wu, wd) = _inproj(
        xf, norm_mix_g[l][None, :], w_all, b_gate[l][None, :], vnorm_g[l][None, :], qg, kg,
        _mean_matrix(B_WIDTH), _mean_matrix(KV_DUP), (w_gate_e[l], w_up_e[l], w_down_e[l]))
    x1, aff = _mix(B, S, u, vn, q, k, v, ga, gb, xf, wsp, bsp, bias, sink,
                   w_proj_a[l].astype(BF16), w_proj_b[l].astype(BF16), w_out[l].astype(BF16),
                   norm_ffn_g[l][None, :], wr2)
    afft = aff.reshape(B, S, N_EXPERTS).transpose(0, 2, 1).reshape(B * N_EXPERTS, S)
    idx, gates = _topk(B, S, cap, afft, tri)
    out = _moe(B, S, cap, idx.reshape(-1), x1, gates.reshape(B, N_EXPERTS, cap, 1), wg, wu, wd,
               norm_ffn_g[l][None, :])
    return out.reshape(B, S, D_MODEL)
```

```python
import functools

import numpy as np
import jax
import jax.numpy as jnp
from jax import lax
from jax.experimental import pallas as pl
from jax.experimental.pallas import tpu as pltpu

F32 = jnp.float32
BF16 = jnp.bfloat16

D_MODEL = 1024
A_GROUPS = 8
A_GROUP_DIM = 64
A_WIDTH = A_GROUPS * A_GROUP_DIM
CHUNK = 128
N_HEADS = 8
N_KV_HEADS = 2
HEAD_DIM = 64
Q_GROUP = N_HEADS // N_KV_HEADS
B_WIDTH = N_HEADS * HEAD_DIM
KV_WIDTH = N_KV_HEADS * HEAD_DIM
WINDOW = 128
BLOCK = 128
SPAN = BLOCK + 2 * WINDOW
N_BUCKETS = 32
MAX_DISTANCE = 128
N_EXPERTS = 16
EXPERT_FF = 2048
CAPACITY_FACTOR = 2
EPS = 1e-6

LANES = 128
KV_DUP = 2 * KV_WIDTH
TM_INPROJ = 512
TQ_MIX = 512
FF_CHUNK = 256
MOE_GROUP = 2
SCATTER_ROWS = 8
TOPK_BATCHES = 8
TOPK_ROWS = 8
VMEM_LIMIT = 56 * 1024 * 1024
INF_BITS = 0x7F800000


def _rms(x, g):
    return x * lax.rsqrt(jnp.mean(x * x, axis=-1, keepdims=True) + EPS) * g


def _inproj_kernel(x_ref, g_ref, w_ref, bg_ref, vg_ref, qg_ref, kg_ref, bdq_ref, bdk_ref,
                   ewg_ref, ewu_ref, ewd_ref,
                   u_ref, vn_ref, q_ref, k_ref, v_ref, ga_ref, gb_ref, bwg_ref, bwu_ref, bwd_ref):
    bwg_ref[...] = ewg_ref[...].astype(BF16)
    bwu_ref[...] = ewu_ref[...].astype(BF16)
    bwd_ref[...] = ewd_ref[...].astype(BF16)

    hb = _rms(x_ref[...], g_ref[...]).astype(BF16)

    def seg(lo, hi):
        return jnp.dot(hb, w_ref[:, lo:hi], preferred_element_type=F32)

    o_va, o_q, o_k = A_WIDTH, 2 * A_WIDTH, 2 * A_WIDTH + B_WIDTH
    o_ga = o_k + 2 * KV_DUP
    o_gb = o_ga + D_MODEL
    ga_ref[...] = jax.nn.sigmoid(seg(o_ga, o_gb) + bg_ref[:, :D_MODEL]).astype(BF16)
    gb_ref[...] = jax.nn.sigmoid(seg(o_gb, o_gb + D_MODEL) + bg_ref[:, D_MODEL:]).astype(BF16)
    zq = seg(o_q, o_k)
    msq = jnp.dot((zq * zq).astype(BF16), bdq_ref[...], preferred_element_type=F32)
    q_ref[...] = (zq * lax.rsqrt(msq + EPS) * qg_ref[...]).astype(BF16)
    u_ref[...] = jax.nn.gelu(seg(0, o_va)).astype(BF16)
    gv = jax.nn.gelu(seg(o_va, o_q))
    vn_ref[...] = _rms(gv, vg_ref[...]).astype(BF16)
    zkv = seg(o_k, o_ga)
    zk = zkv[:, :KV_DUP]
    msk = jnp.dot((zk * zk).astype(BF16), bdk_ref[...], preferred_element_type=F32)
    k_ref[...] = (zk * lax.rsqrt(msk + EPS) * kg_ref[...]).astype(BF16)
    v_ref[...] = zkv[:, KV_DUP:].astype(BF16)


def _inproj(xf, norm_g, w_all, b_gate, vnorm_g, qg, kg, bdq, bdk, expert_w):
    T = xf.shape[0]
    tm = TM_INPROJ
    steps = T // tm
    ncol = w_all.shape[1]
    row = lambda w: pl.BlockSpec((tm, w), lambda i: (i, 0))
    full = lambda a: pl.BlockSpec(a.shape, lambda i: (0,) * a.ndim)
    out_w = (A_WIDTH, A_WIDTH, B_WIDTH, KV_DUP, KV_DUP, D_MODEL, D_MODEL)
    flat = [w.reshape(-1, w.shape[-1]) for w in expert_w]
    assert all(w.shape[0] % (steps * 16) == 0 for w in flat)
    slab = lambda w: pl.BlockSpec((w.shape[0] // steps, w.shape[1]), lambda i: (i, 0))
    outs = pl.pallas_call(
        _inproj_kernel,
        grid=(steps,),
        in_specs=[row(D_MODEL), full(norm_g), pl.BlockSpec((D_MODEL, ncol), lambda i: (0, 0)),
                  full(b_gate), full(vnorm_g), full(qg), full(kg), full(bdq), full(bdk)]
                 + [slab(w) for w in flat],
        out_specs=[row(w) for w in out_w] + [slab(w) for w in flat],
        out_shape=[jax.ShapeDtypeStruct((T, w), BF16) for w in out_w]
                  + [jax.ShapeDtypeStruct(w.shape, BF16) for w in flat],
        compiler_params=pltpu.CompilerParams(dimension_semantics=("arbitrary",),
                                             vmem_limit_bytes=VMEM_LIMIT),
        name="inproj",
    )(xf, norm_g, w_all, b_gate, vnorm_g, qg, kg, bdq, bdk, *flat)
    return outs[:len(out_w)], [o.reshape(w.shape) for o, w in zip(outs[len(out_w):], expert_w)]


def _block_diag_pair(slab, lo_mask):
    zero = jnp.zeros_like(slab)
    return jnp.concatenate([jnp.where(lo_mask, slab, zero), jnp.where(lo_mask, zero, slab)], axis=0)


def _mix_kernel(u_ref, vn_ref, q_ref, kc_ref, kp_ref, kn_ref, vc_ref, vp_ref, vx_ref,
                ga_ref, gb_ref, x_ref, wsp_ref, bsp_ref, bias_ref, sink_ref,
                wpa_ref, wpb_ref, wo_ref, ng_ref, wr_ref,
                x1_ref, aff_ref, kbuf, vbuf, abuf, obuf):
    tq = x_ref.shape[0]
    nblk = tq // BLOCK
    i = pl.program_id(1)
    last = pl.num_programs(1) - 1
    lo_mask = lax.broadcasted_iota(jnp.int32, (1, LANES), 1) < HEAD_DIM

    kbuf[0:WINDOW, :] = kp_ref[...]
    kbuf[WINDOW:WINDOW + tq, :] = kc_ref[...]
    kbuf[WINDOW + tq:, :] = kn_ref[...]
    vbuf[0:WINDOW, :] = vp_ref[...]
    vbuf[WINDOW:WINDOW + tq, :] = vc_ref[...]
    vbuf[WINDOW + tq:, :] = vx_ref[...]

    def spatial_gating(c):
        rows = slice(c * CHUNK, (c + 1) * CHUNK)
        parts = []
        for j in range(A_GROUPS // 2):
            bd = _block_diag_pair(vn_ref[rows, j * LANES:(j + 1) * LANES], lo_mask)
            parts.append(jnp.dot(wsp_ref[j], bd, preferred_element_type=F32))
        mixed = jnp.concatenate(parts, axis=1) + bsp_ref[...]
        abuf[rows, :] = (u_ref[rows, :].astype(F32) * mixed).astype(BF16)

    vrow = lax.broadcasted_iota(jnp.int32, (2 * SPAN, LANES), 0)
    vcol = lax.broadcasted_iota(jnp.int32, (2 * SPAN, LANES), 1)
    ones_cols = jnp.where((vrow < SPAN) == (vcol < HEAD_DIM), 1.0, 0.0).astype(BF16)

    def attention_stages(blocks):
        units = [(n, kh) for n in blocks for kh in range(N_KV_HEADS)]
        st = {}

        def geometry(n, kh):
            if n == 0:
                var = jnp.where(i == 0, 0, 1)
            elif n == nblk - 1:
                var = jnp.where(i == last, 2, 1)
            else:
                var = 1
            return (slice(n * BLOCK, (n + 1) * BLOCK), slice(n * BLOCK, n * BLOCK + SPAN),
                    slice(kh * LANES, (kh + 1) * LANES), var)

        def scores():
            for n, kh in units:
                rows, span, ksl, var = geometry(n, kh)
                kk = _block_diag_pair(kbuf[span, ksl], lo_mask)
                q2 = jnp.concatenate([q_ref[rows, (2 * kh) * LANES:(2 * kh + 1) * LANES],
                                      q_ref[rows, (2 * kh + 1) * LANES:(2 * kh + 2) * LANES]], axis=0)
                st["s", n, kh] = (lax.dot_general(q2, kk, (((1,), (1,)), ((), ())),
                                                  preferred_element_type=F32)
                                  + bias_ref[var, kh])

        def numerators():
            for n, kh in units:
                p2, m2 = [], []
                for half in range(2):
                    sh = st["s", n, kh][:, half * SPAN:(half + 1) * SPAN]
                    m = jnp.maximum(jnp.max(sh, axis=-1, keepdims=True), sink_ref[kh, half])
                    p2.append(jnp.exp(sh - m).astype(BF16))
                    m2.append(m)
                st["p", n, kh] = jnp.concatenate(p2, axis=1)
                st["m", n, kh] = m2

        def values():
            for n, kh in units:
                rows, span, ksl, _ = geometry(n, kh)
                vv = jnp.concatenate([_block_diag_pair(vbuf[span, ksl], lo_mask), ones_cols], axis=1)
                o2 = jnp.dot(st["p", n, kh], vv, preferred_element_type=F32)
                m2 = st["m", n, kh]
                sink_term = jnp.where(lo_mask, jnp.exp(sink_ref[kh, 0] - m2[0]),
                                      jnp.exp(sink_ref[kh, 1] - m2[1]))
                o = o2[:, :LANES] * (1.0 / (o2[:, LANES:] + sink_term))
                obuf[rows, (2 * kh) * LANES:(2 * kh + 1) * LANES] = o[:BLOCK].astype(BF16)
                obuf[rows, (2 * kh + 1) * LANES:(2 * kh + 2) * LANES] = o[BLOCK:].astype(BF16)

        return [scores, numerators, values]

    def projection_pieces(rows):
        st = {}

        def branches():
            pa = jnp.dot(abuf[rows, :], wpa_ref[...], preferred_element_type=F32)
            pb = jnp.dot(obuf[rows, :], wpb_ref[...], preferred_element_type=F32)
            st["merged"] = (ga_ref[rows, :].astype(F32) * pa + gb_ref[rows, :].astype(F32) * pb).astype(BF16)

        def output():
            st["x1"] = x_ref[rows, :] + jnp.dot(st["merged"], wo_ref[...], preferred_element_type=F32)
            x1_ref[rows, :] = st["x1"]

        def router():
            h2 = _rms(st["x1"], ng_ref[...])
            hi = h2.astype(BF16)
            lo = (h2 - hi.astype(F32)).astype(BF16)
            both = jnp.dot(hi, wr_ref[...], preferred_element_type=F32)
            logits = (both[:, :LANES] + jnp.dot(lo, wr_ref[:, :LANES], preferred_element_type=F32)
                      + both[:, LANES:])[:, :N_EXPERTS]
            ex = jnp.exp(logits - jnp.max(logits, axis=-1, keepdims=True))
            aff_ref[rows, :] = ex / jnp.sum(ex, axis=-1, keepdims=True)

        return [branches, output, router]

    for c in range(nblk // 2):
        spatial_gating(c)
    later = list(range(nblk // 2, nblk))
    for stage in attention_stages(range(nblk // 2)):
        stage()
        if later:
            spatial_gating(later.pop(0))
    assert not later
    for stage, piece in zip(attention_stages(range(nblk // 2, nblk)),
                            projection_pieces(slice(0, tq // 2))):
        stage()
        piece()
    for piece in projection_pieces(slice(tq // 2, tq)):
        piece()


def _mix(B, S, u, vn, q, k, v, ga, gb, xf, wsp, bsp, bias, sink, wpa, wpb, wo, ng, wr2):
    T = B * S
    tq = TQ_MIX
    nq = S // tq
    bpq = tq // BLOCK
    nb = S // BLOCK
    row = lambda w: pl.BlockSpec((tq, w), lambda b, i: (b * nq + i, 0))
    prev = pl.BlockSpec((BLOCK, KV_DUP), lambda b, i: (b * nb + jnp.maximum(i * bpq - 1, 0), 0))
    nxt = pl.BlockSpec((BLOCK, KV_DUP), lambda b, i: (b * nb + jnp.minimum(i * bpq + bpq, nb - 1), 0))
    full = lambda a: pl.BlockSpec(a.shape, lambda b, i: (0,) * a.ndim)
    return pl.pallas_call(
        _mix_kernel,
        grid=(B, nq),
        in_specs=[row(A_WIDTH), row(A_WIDTH), row(B_WIDTH),
                  row(KV_DUP), prev, nxt, row(KV_DUP), prev, nxt,
                  row(D_MODEL), row(D_MODEL), row(D_MODEL),
                  full(wsp), full(bsp), full(bias), full(sink),
                  full(wpa), full(wpb), full(wo), full(ng), full(wr2)],
        out_specs=[row(D_MODEL), row(N_EXPERTS)],
        out_shape=[jax.ShapeDtypeStruct((T, D_MODEL), F32),
                   jax.ShapeDtypeStruct((T, N_EXPERTS), F32)],
        scratch_shapes=[pltpu.VMEM((tq + 2 * WINDOW, KV_DUP), BF16),
                        pltpu.VMEM((tq + 2 * WINDOW, KV_DUP), BF16),
                        pltpu.VMEM((tq, A_WIDTH), BF16),
                        pltpu.VMEM((tq, B_WIDTH), BF16)],
        compiler_params=pltpu.CompilerParams(dimension_semantics=("arbitrary", "arbitrary"),
                                             vmem_limit_bytes=VMEM_LIMIT),
        name="mix",
    )(u, vn, q, k, k, k, v, v, v, ga, gb, xf, wsp, bsp, bias, sink, wpa, wpb, wo, ng, wr2)


def _topk_kernel(afft_ref, tri_ref, idx_ref, gate_ref, cum, cend, *, cap):
    R, S = afft_ref.shape
    nblk = S // LANES
    j = pl.program_id(1)
    lane = lax.broadcasted_iota(jnp.int32, (1, LANES), 1)

    @pl.when(j == 0)
    def _():
        bits = pltpu.bitcast(afft_ref[...], jnp.int32)

        def search(_, carry):
            lo, hi = carry
            mid = lo + ((hi - lo) >> 1)
            cnt = jnp.sum(jnp.where(bits >= mid, 1.0, 0.0), axis=1, keepdims=True)
            ge = cnt >= cap
            return jnp.where(ge, mid, lo), jnp.where(ge, hi, mid)

        lo0 = jnp.zeros((R, 1), jnp.int32)
        hi0 = jnp.full((R, 1), INF_BITS, jnp.int32)
        thr, _ = lax.fori_loop(0, 31, search, (lo0, hi0))

        def prefix(mask):
            outs, carry, ends = [], jnp.zeros((R, 1), F32), jnp.zeros((R, LANES), F32)
            for jj in range(nblk):
                mj = mask[:, jj * LANES:(jj + 1) * LANES]
                outs.append(jnp.dot(mj.astype(BF16), tri_ref[...], preferred_element_type=F32) + carry)
                carry = carry + jnp.sum(mj, axis=1, keepdims=True)
                ends = jnp.where(lane == jj, carry, ends)
            return jnp.concatenate(outs, axis=1), ends

        gt = jnp.where(bits > thr, 1.0, 0.0)
        eq = jnp.where(bits == thr, 1.0, 0.0)
        need = cap - jnp.sum(gt, axis=1, keepdims=True)
        sel = gt + eq * jnp.where(prefix(eq)[0] < need, 1.0, 0.0)
        before, ends = prefix(sel)
        cum[...] = before + sel
        cend[...] = ends

    slot = lax.broadcasted_iota(jnp.int32, (cap, 1), 0).astype(F32)

    def by_block(row):
        return jnp.concatenate([row[:, b * LANES:(b + 1) * LANES] for b in range(nblk)], axis=0)

    def compact(rr):
        r = j * TOPK_ROWS + rr
        done = jnp.where((cend[pl.ds(r, 1), :] <= slot) & (lane < nblk), 1.0, 0.0)
        blk = jnp.sum(done, axis=1, keepdims=True).astype(jnp.int32)
        pick = jnp.where(lane == blk, 1.0, 0.0).astype(BF16)[:, :nblk]
        cum_b = jnp.dot(pick, by_block(cum[pl.ds(r, 1), :]).astype(BF16), preferred_element_type=F32)
        off = jnp.sum(jnp.where(cum_b <= slot, 1.0, 0.0), axis=1, keepdims=True).astype(jnp.int32)
        idx_ref[rr] = blk * LANES + off
        a = by_block(afft_ref[pl.ds(r, 1), :])
        a_hi = a.astype(BF16)
        r1 = a - a_hi.astype(F32)
        a_mid = r1.astype(BF16)
        a_lo = (r1 - a_mid.astype(F32)).astype(BF16)
        aff_b = (jnp.dot(pick, a_hi, preferred_element_type=F32)
                 + jnp.dot(pick, a_mid, preferred_element_type=F32)
                 + jnp.dot(pick, a_lo, preferred_element_type=F32))
        gate_ref[rr] = jnp.sum(jnp.where(lane == off, aff_b, 0.0), axis=1, keepdims=True)

    for rr in range(TOPK_ROWS):
        compact(rr)


def _topk(B, S, cap, afft, tri):
    nb = min(TOPK_BATCHES, B)
    R = nb * N_EXPERTS
    steps = R // TOPK_ROWS
    out_block = pl.BlockSpec((TOPK_ROWS, cap, 1), lambda i, j: (i * steps + j, 0, 0))
    return pl.pallas_call(
        functools.partial(_topk_kernel, cap=cap),
        grid=(B // nb, steps),
        in_specs=[pl.BlockSpec((R, S), lambda i, j: (i, 0)),
                  pl.BlockSpec(tri.shape, lambda i, j: (0, 0))],
        out_specs=[out_block, out_block],
        out_shape=[jax.ShapeDtypeStruct((B * N_EXPERTS, cap, 1), jnp.int32),
                   jax.ShapeDtypeStruct((B * N_EXPERTS, cap, 1), F32)],
        scratch_shapes=[pltpu.VMEM((R, S), F32), pltpu.VMEM((R, LANES), F32)],
        compiler_params=pltpu.CompilerParams(dimension_semantics=("arbitrary", "arbitrary"),
                                             vmem_limit_bytes=VMEM_LIMIT),
        name="topk",
    )(afft, tri)


def _moe_kernel(idx_ref, x1_hbm, gate_ref, wg_ref, wu_ref, wd_ref, ng_ref, out_hbm,
                acc, hbuf, xg, ybuf, sem_in, sem, *, S, cap):
    g = pl.program_id(0)
    e2 = pl.program_id(1)
    k = pl.program_id(2)
    nexp = 2 * pl.num_programs(1)
    half = D_MODEL // 2
    rows0 = g * (MOE_GROUP * S)
    hi_mask = jnp.int32(-65536)

    def lists(seq, ee):
        return ((g * MOE_GROUP + seq) * nexp + ee) * cap

    def gather_jobs(seq, ee, par):
        base = lists(seq, ee)

        def job(c0):
            for c in range(c0, c0 + SCATTER_ROWS):
                r = seq * S + idx_ref[base + c]
                xg[par, seq * cap + c:seq * cap + c + 1, :] = hbuf[pl.ds(r, 1), :]
        return [functools.partial(job, c0) for c0 in range(0, cap, SCATTER_ROWS)]

    def scatter_jobs(seq, ee, par):
        base = lists(seq, ee)

        def job(c0):
            rows = [seq * S + idx_ref[base + c] for c in range(c0, c0 + SCATTER_ROWS)]
            old = [acc[pl.ds(r, 1), :] for r in rows]
            for j, r in enumerate(rows):
                c = seq * cap + c0 + j
                acc[pl.ds(r, 1), :] = old[j] + ybuf[par, c:c + 1, :]
        return [functools.partial(job, c0) for c0 in range(0, cap, SCATTER_ROWS)]

    def ffn(par, accumulate, jobs):
        n_up, n_down = wg_ref.shape[2] // FF_CHUNK, D_MODEL // FF_CHUNK
        unit = -(-len(jobs) // (2 * n_up + n_down - 1))
        jobs = list(jobs)

        def run_share(units):
            for job in jobs[:units * unit]:
                job()
            del jobs[:units * unit]

        p = xg[par]
        lo = pltpu.bitcast(p << 16, F32).astype(BF16)
        hi = pltpu.bitcast(p & hi_mask, F32).astype(BF16)
        x = jnp.concatenate([lo, hi], axis=1)
        hid = []
        for f in range(n_up):
            cols = slice(f * FF_CHUNK, (f + 1) * FF_CHUNK)
            gt = jnp.dot(x, wg_ref[0, :, cols], preferred_element_type=F32)
            up = jnp.dot(x, wu_ref[0, :, cols], preferred_element_type=F32)
            hid.append((gt * jax.nn.sigmoid(gt) * up).astype(BF16))
            run_share(2)
        hid = jnp.concatenate(hid, axis=1)
        gate = jnp.concatenate([gate_ref[s, 0] for s in range(MOE_GROUP)], axis=0)
        for n in range(n_down):
            cols = slice(n * FF_CHUNK, (n + 1) * FF_CHUNK)
            y = jnp.dot(hid, wd_ref[0, :, cols], preferred_element_type=F32) * gate
            ybuf[par, :, cols] = ybuf[par, :, cols] + y if accumulate else y
            run_share(1 if n < n_down - 1 else 0)
        assert not jobs

    @pl.when((e2 == 0) & (k == 0))
    def _():
        def load(i):
            start = i * cap if isinstance(i, int) else pl.multiple_of(i * cap, cap)
            return pltpu.make_async_copy(x1_hbm.at[pl.ds(rows0 + start, cap), :],
                                         acc.at[pl.ds(start, cap), :], sem_in.at[i])

        for i in range(sem_in.shape[0]):
            load(i).start()

        def pack(i, _):
            load(i).wait()
            rows = pl.ds(pl.multiple_of(i * cap, cap), cap)
            hb = _rms(acc[rows, :], ng_ref[...]).astype(BF16).astype(F32)
            u = pltpu.bitcast(hb, jnp.int32)
            hbuf[rows, :] = lax.shift_right_logical(u[:, :half], 16) | (u[:, half:] & hi_mask)
            return 0
        lax.fori_loop(0, MOE_GROUP * S // cap, pack, 0)
        ybuf[1] = jnp.zeros(ybuf.shape[1:], F32)
        for seq in range(MOE_GROUP):
            for job in gather_jobs(seq, 0, 0):
                job()

    for kk in range(4):
        @pl.when(k == kk)
        def _(par=kk // 2, fh=kk % 2):
            e = 2 * e2 + par
            sc = scatter_jobs(fh, jnp.maximum(e - 1, 0), 1 - par)
            ga = gather_jobs(fh, jnp.minimum(e + 1, nexp - 1), 1 - par)
            ffn(par, fh == 1, [job for pair in zip(sc, ga) for job in pair])

    @pl.when((e2 == pl.num_programs(1) - 1) & (k == 3))
    def _():
        for seq in range(MOE_GROUP):
            for job in scatter_jobs(seq, nexp - 1, 1):
                job()
        cp = pltpu.make_async_copy(acc, out_hbm.at[pl.ds(rows0, MOE_GROUP * S), :], sem)
        cp.start()
        cp.wait()


def _moe(B, S, cap, idx_flat, x1, gates, wg, wu, wd, ng):
    G = MOE_GROUP
    T = B * S
    fhw = EXPERT_FF // 2
    ex = lambda e2, k: 2 * e2 + k // 2
    fh = lambda k: k % 2
    grid_spec = pltpu.PrefetchScalarGridSpec(
        num_scalar_prefetch=1,
        grid=(B // G, N_EXPERTS // 2, 4),
        in_specs=[pl.BlockSpec(memory_space=pl.ANY),
                  pl.BlockSpec((G, 1, cap, 1), lambda g, e2, k, idx: (g, ex(e2, k), 0, 0)),
                  pl.BlockSpec((1, D_MODEL, fhw), lambda g, e2, k, idx: (ex(e2, k), 0, fh(k))),
                  pl.BlockSpec((1, D_MODEL, fhw), lambda g, e2, k, idx: (ex(e2, k), 0, fh(k))),
                  pl.BlockSpec((1, fhw, D_MODEL), lambda g, e2, k, idx: (ex(e2, k), fh(k), 0)),
                  pl.BlockSpec((1, D_MODEL), lambda g, e2, k, idx: (0, 0))],
        out_specs=pl.BlockSpec(memory_space=pl.ANY),
        scratch_shapes=[pltpu.VMEM((G * S, D_MODEL), F32),
                        pltpu.VMEM((G * S, D_MODEL // 2), jnp.int32),
                        pltpu.VMEM((2, G * cap, D_MODEL // 2), jnp.int32),
                        pltpu.VMEM((2, G * cap, D_MODEL), F32),
                        pltpu.SemaphoreType.DMA((G * S // cap,)),
                        pltpu.SemaphoreType.DMA(())],
    )
    return pl.pallas_call(
        functools.partial(_moe_kernel, S=S, cap=cap),
        grid_spec=grid_spec,
        out_shape=jax.ShapeDtypeStruct((T, D_MODEL), F32),
        input_output_aliases={1: 0},
        compiler_params=pltpu.CompilerParams(dimension_semantics=("arbitrary",) * 3,
                                             vmem_limit_bytes=VMEM_LIMIT),
        name="moe",
    )(idx_flat, x1, gates, wg, wu, wd, ng)


def _t5_bucket(rel):
    nb = N_BUCKETS // 2
    max_exact = nb // 2
    ret = (rel > 0).astype(np.int32) * nb
    n = np.abs(rel)
    large = max_exact + (np.log(np.maximum(n, 1) / max_exact) / np.log(MAX_DISTANCE / max_exact)
                         * (nb - max_exact)).astype(np.int32)
    large = np.minimum(large, nb - 1)
    return (ret + np.where(n < max_exact, n, large)).astype(np.int32)


def _bias_kernel(table_ref, bucket_ref, out_ref):
    bucket = bucket_ref[...]
    row = lax.broadcasted_iota(jnp.int32, (BLOCK, SPAN), 0)
    col = lax.broadcasted_iota(jnp.int32, (BLOCK, SPAN), 1)
    band = jnp.abs(col - WINDOW - row) <= WINDOW
    valid = (band & (col >= WINDOW), band, band & (col < WINDOW + BLOCK))
    for h in range(N_HEADS):
        b = jnp.zeros((BLOCK, SPAN), F32)
        for n in range(N_BUCKETS):
            b = jnp.where(bucket == n, table_ref[n, h], b)
        kh, slab, half = h // Q_GROUP, (h % Q_GROUP) // 2, h % 2
        for var in range(3):
            out_ref[var, kh, slab * BLOCK:(slab + 1) * BLOCK, half * SPAN:(half + 1) * SPAN] = (
                jnp.where(valid[var], b, -jnp.inf))


def _attention_bias(rel_table):
    rel = (np.arange(SPAN)[None, :] - WINDOW) - np.arange(BLOCK)[:, None]
    bucket = jnp.asarray(_t5_bucket(rel))
    return pl.pallas_call(
        _bias_kernel,
        in_specs=[pl.BlockSpec(memory_space=pltpu.SMEM),
                  pl.BlockSpec(bucket.shape, lambda: (0, 0))],
        out_specs=pl.BlockSpec((3, N_KV_HEADS, 2 * BLOCK, 2 * SPAN), lambda: (0, 0, 0, 0)),
        out_shape=jax.ShapeDtypeStruct((3, N_KV_HEADS, 2 * BLOCK, 2 * SPAN), F32),
        name="t5bias",
    )(rel_table.astype(F32), bucket)


def _dup_heads(w):
    parts = []
    for h in range(N_KV_HEADS):
        wh = w[..., h * HEAD_DIM:(h + 1) * HEAD_DIM]
        parts += [wh, wh]
    return jnp.concatenate(parts, axis=-1)


def _mean_matrix(width):
    blk = np.arange(width) // HEAD_DIM
    return jnp.asarray((blk[:, None] == blk[None, :]).astype(np.float32) / HEAD_DIM, dtype=BF16)


def kernel(x, norm_mix_g, w_in, b_gate, vnorm_g, w_spatial, b_spatial, q_norm_g, k_norm_g,
           attn_sink, rel_bias_table, w_proj_a, w_proj_b, w_out, norm_ffn_g, w_router,
           w_gate_e, w_up_e, w_down_e):
    B, S, _ = x.shape
    T = B * S
    cap = CAPACITY_FACTOR * S // N_EXPERTS
    assert S % TQ_MIX == 0 and T % TM_INPROJ == 0 and B % MOE_GROUP == 0
    assert B % min(TOPK_BATCHES, B) == 0
    xf = x.reshape(T, D_MODEL)
    l = 0

    w = w_in[l]
    o_q = 2 * A_WIDTH
    o_k = o_q + B_WIDTH
    o_v = o_k + KV_WIDTH
    o_g = o_v + KV_WIDTH
    w_all = jnp.concatenate([w[:, :o_k], _dup_heads(w[:, o_k:o_v]), _dup_heads(w[:, o_v:o_g]),
                             w[:, o_g:]], axis=1).astype(BF16)
    qg = jnp.tile(q_norm_g[l], N_HEADS)[None, :] * (HEAD_DIM ** -0.5)
    kg = jnp.tile(k_norm_g[l], 2 * N_KV_HEADS)[None, :]
    wsp = w_spatial[l].reshape(A_GROUPS // 2, 2, CHUNK, CHUNK).transpose(0, 2, 1, 3)
    wsp = wsp.reshape(A_GROUPS // 2, CHUNK, 2 * CHUNK).astype(BF16)
    bsp = jnp.repeat(b_spatial[l].T, A_GROUP_DIM, axis=1)
    bias = _attention_bias(rel_bias_table)
    sink = attn_sink[l].astype(F32).reshape(N_KV_HEADS, 2, 2)
    sink = jnp.broadcast_to(sink.transpose(0, 2, 1)[:, :, :, None, None],
                            (N_KV_HEADS, 2, 2, BLOCK, 1)).reshape(N_KV_HEADS, 2, 2 * BLOCK, 1)
    wr = w_router[l]
    wrh = wr.astype(BF16)
    wrl = (wr - wrh.astype(F32)).astype(BF16)
    pad = jnp.zeros((D_MODEL, LANES - N_EXPERTS), BF16)
    wr2 = jnp.concatenate([wrh, pad, wrl, pad], axis=1)
    tri = jnp.asarray(np.triu(np.ones((LANES, LANES), np.float32), k=1), dtype=BF16)

    (u, vn, q, k, v, ga, gb), (wg, wu, wd) = _inproj(
        xf, norm_mix_g[l][None, :], w_all, b_gate[l][None, :], vnorm_g[l][None, :], qg, kg,
        _mean_matrix(B_WIDTH), _mean_matrix(KV_DUP), (w_gate_e[l], w_up_e[l], w_down_e[l]))
    x1, aff = _mix(B, S, u, vn, q, k, v, ga, gb, xf, wsp, bsp, bias, sink,
                   w_proj_a[l].astype(BF16), w_proj_b[l].astype(BF16), w_out[l].astype(BF16),
                   norm_ffn_g[l][None, :], wr2)
    afft = aff.reshape(B, S, N_EXPERTS).transpose(0, 2, 1).reshape(B * N_EXPERTS, S)
    idx, gates = _topk(B, S, cap, afft, tri)
    out = _moe(B, S, cap, idx.reshape(-1), x1, gates.reshape(B, N_EXPERTS, cap, 1), wg, wu, wd,
               norm_ffn_g[l][None, :])
    return out.reshape(B, S, D_MODEL)
```

```python
import functools

import numpy as np
import jax
import jax.numpy as jnp
from jax import lax
from jax.experimental import pallas as pl
from jax.experimental.pallas import tpu as pltpu

F32 = jnp.float32
BF16 = jnp.bfloat16

D_MODEL = 1024
A_GROUPS = 8
A_GROUP_DIM = 64
A_WIDTH = A_GROUPS * A_GROUP_DIM
CHUNK = 128
N_HEADS = 8
N_KV_HEADS = 2
HEAD_DIM = 64
Q_GROUP = N_HEADS // N_KV_HEADS
B_WIDTH = N_HEADS * HEAD_DIM
KV_WIDTH = N_KV_HEADS * HEAD_DIM
WINDOW = 128
BLOCK = 128
SPAN = BLOCK + 2 * WINDOW
N_BUCKETS = 32
MAX_DISTANCE = 128
N_EXPERTS = 16
EXPERT_FF = 2048
CAPACITY_FACTOR = 2
EPS = 1e-6

LANES = 128
KV_DUP = 2 * KV_WIDTH
TM_INPROJ = 512
TQ_MIX = 1024
FF_CHUNK = 256
MOE_GROUP = 2
SCATTER_ROWS = 8
TOPK_BATCHES = 8
TOPK_ROWS = 8
VMEM_LIMIT = 56 * 1024 * 1024
INF_BITS = 0x7F800000


def _rms(x, g):
    return x * lax.rsqrt(jnp.mean(x * x, axis=-1, keepdims=True) + EPS) * g


def _inproj_kernel(x_ref, g_ref, w_ref, bg_ref, vg_ref, qg_ref, kg_ref, bdq_ref, bdk_ref,
                   ewg_ref, ewu_ref, ewd_ref,
                   u_ref, vn_ref, q_ref, k_ref, v_ref, ga_ref, gb_ref, bwg_ref, bwu_ref, bwd_ref):
    bwg_ref[...] = ewg_ref[...].astype(BF16)
    bwu_ref[...] = ewu_ref[...].astype(BF16)
    bwd_ref[...] = ewd_ref[...].astype(BF16)

    hb = _rms(x_ref[...], g_ref[...]).astype(BF16)

    def seg(lo, hi):
        return jnp.dot(hb, w_ref[:, lo:hi], preferred_element_type=F32)

    o_va, o_q, o_k = A_WIDTH, 2 * A_WIDTH, 2 * A_WIDTH + B_WIDTH
    o_ga = o_k + 2 * KV_DUP
    o_gb = o_ga + D_MODEL
    ga_ref[...] = jax.nn.sigmoid(seg(o_ga, o_gb) + bg_ref[:, :D_MODEL]).astype(BF16)
    gb_ref[...] = jax.nn.sigmoid(seg(o_gb, o_gb + D_MODEL) + bg_ref[:, D_MODEL:]).astype(BF16)
    zq = seg(o_q, o_k)
    msq = jnp.dot((zq * zq).astype(BF16), bdq_ref[...], preferred_element_type=F32)
    q_ref[...] = (zq * lax.rsqrt(msq + EPS) * qg_ref[...]).astype(BF16)
    u_ref[...] = jax.nn.gelu(seg(0, o_va)).astype(BF16)
    gv = jax.nn.gelu(seg(o_va, o_q))
    vn_ref[...] = _rms(gv, vg_ref[...]).astype(BF16)
    zkv = seg(o_k, o_ga)
    zk = zkv[:, :KV_DUP]
    msk = jnp.dot((zk * zk).astype(BF16), bdk_ref[...], preferred_element_type=F32)
    k_ref[...] = (zk * lax.rsqrt(msk + EPS) * kg_ref[...]).astype(BF16)
    v_ref[...] = zkv[:, KV_DUP:].astype(BF16)


def _inproj(xf, norm_g, w_all, b_gate, vnorm_g, qg, kg, bdq, bdk, expert_w):
    T = xf.shape[0]
    tm = TM_INPROJ
    steps = T // tm
    ncol = w_all.shape[1]
    row = lambda w: pl.BlockSpec((tm, w), lambda i: (i, 0))
    full = lambda a: pl.BlockSpec(a.shape, lambda i: (0,) * a.ndim)
    out_w = (A_WIDTH, A_WIDTH, B_WIDTH, KV_DUP, KV_DUP, D_MODEL, D_MODEL)
    flat = [w.reshape(-1, w.shape[-1]) for w in expert_w]
    assert all(w.shape[0] % (steps * 16) == 0 for w in flat)
    slab = lambda w: pl.BlockSpec((w.shape[0] // steps, w.shape[1]), lambda i: (i, 0))
    outs = pl.pallas_call(
        _inproj_kernel,
        grid=(steps,),
        in_specs=[row(D_MODEL), full(norm_g), pl.BlockSpec((D_MODEL, ncol), lambda i: (0, 0)),
                  full(b_gate), full(vnorm_g), full(qg), full(kg), full(bdq), full(bdk)]
                 + [slab(w) for w in flat],
        out_specs=[row(w) for w in out_w] + [slab(w) for w in flat],
        out_shape=[jax.ShapeDtypeStruct((T, w), BF16) for w in out_w]
                  + [jax.ShapeDtypeStruct(w.shape, BF16) for w in flat],
        compiler_params=pltpu.CompilerParams(dimension_semantics=("arbitrary",),
                                             vmem_limit_bytes=VMEM_LIMIT),
        name="inproj",
    )(xf, norm_g, w_all, b_gate, vnorm_g, qg, kg, bdq, bdk, *flat)
    return outs[:len(out_w)], [o.reshape(w.shape) for o, w in zip(outs[len(out_w):], expert_w)]


def _block_diag_pair(slab, lo_mask):
    zero = jnp.zeros_like(slab)
    return jnp.concatenate([jnp.where(lo_mask, slab, zero), jnp.where(lo_mask, zero, slab)], axis=0)


def _mix_kernel(u_ref, vn_ref, q_ref, kc_ref, kp_ref, kn_ref, vc_ref, vp_ref, vx_ref,
                ga_ref, gb_ref, x_ref, wsp_ref, bsp_ref, bias_ref, sink_ref,
                wpa_ref, wpb_ref, wo_ref, ng_ref, wr_ref,
                x1_ref, aff_ref, kbuf, vbuf, abuf, obuf):
    tq = x_ref.shape[0]
    nblk = tq // BLOCK
    i = pl.program_id(1)
    last = pl.num_programs(1) - 1
    lo_mask = lax.broadcasted_iota(jnp.int32, (1, LANES), 1) < HEAD_DIM

    kbuf[0:WINDOW, :] = kp_ref[...]
    kbuf[WINDOW:WINDOW + tq, :] = kc_ref[...]
    kbuf[WINDOW + tq:, :] = kn_ref[...]
    vbuf[0:WINDOW, :] = vp_ref[...]
    vbuf[WINDOW:WINDOW + tq, :] = vc_ref[...]
    vbuf[WINDOW + tq:, :] = vx_ref[...]

    def spatial_gating(c):
        rows = slice(c * CHUNK, (c + 1) * CHUNK)
        parts = []
        for j in range(A_GROUPS // 2):
            bd = _block_diag_pair(vn_ref[rows, j * LANES:(j + 1) * LANES], lo_mask)
            parts.append(jnp.dot(wsp_ref[j], bd, preferred_element_type=F32))
        mixed = jnp.concatenate(parts, axis=1) + bsp_ref[...]
        abuf[rows, :] = (u_ref[rows, :].astype(F32) * mixed).astype(BF16)

    vrow = lax.broadcasted_iota(jnp.int32, (2 * SPAN, LANES), 0)
    vcol = lax.broadcasted_iota(jnp.int32, (2 * SPAN, LANES), 1)
    ones_cols = jnp.where((vrow < SPAN) == (vcol < HEAD_DIM), 1.0, 0.0).astype(BF16)

    def attention_stages(blocks):
        units = [(n, kh) for n in blocks for kh in range(N_KV_HEADS)]
        st = {}

        def geometry(n, kh):
            if n == 0:
                var = jnp.where(i == 0, 0, 1)
            elif n == nblk - 1:
                var = jnp.where(i == last, 2, 1)
            else:
                var = 1
            return (slice(n * BLOCK, (n + 1) * BLOCK), slice(n * BLOCK, n * BLOCK + SPAN),
                    slice(kh * LANES, (kh + 1) * LANES), var)

        def scores():
            for n, kh in units:
                rows, span, ksl, var = geometry(n, kh)
                kk = _block_diag_pair(kbuf[span, ksl], lo_mask)
                q2 = jnp.concatenate([q_ref[rows, (2 * kh) * LANES:(2 * kh + 1) * LANES],
                                      q_ref[rows, (2 * kh + 1) * LANES:(2 * kh + 2) * LANES]], axis=0)
                st["s", n, kh] = (lax.dot_general(q2, kk, (((1,), (1,)), ((), ())),
                                                  preferred_element_type=F32)
                                  + bias_ref[var, kh])

        def numerators():
            for n, kh in units:
                p2, m2 = [], []
                for half in range(2):
                    sh = st["s", n, kh][:, half * SPAN:(half + 1) * SPAN]
                    m = jnp.maximum(jnp.max(sh, axis=-1, keepdims=True), sink_ref[kh, half])
                    p2.append(jnp.exp(sh - m).astype(BF16))
                    m2.append(m)
                st["p", n, kh] = jnp.concatenate(p2, axis=1)
                st["m", n, kh] = m2

        def values():
            for n, kh in units:
                rows, span, ksl, _ = geometry(n, kh)
                vv = jnp.concatenate([_block_diag_pair(vbuf[span, ksl], lo_mask), ones_cols], axis=1)
                o2 = jnp.dot(st["p", n, kh], vv, preferred_element_type=F32)
                m2 = st["m", n, kh]
                sink_term = jnp.where(lo_mask, jnp.exp(sink_ref[kh, 0] - m2[0]),
                                      jnp.exp(sink_ref[kh, 1] - m2[1]))
                o = o2[:, :LANES] * (1.0 / (o2[:, LANES:] + sink_term))
                obuf[rows, (2 * kh) * LANES:(2 * kh + 1) * LANES] = o[:BLOCK].astype(BF16)
                obuf[rows, (2 * kh + 1) * LANES:(2 * kh + 2) * LANES] = o[BLOCK:].astype(BF16)

        return [scores, numerators, values]

    def projection_pieces(rows):
        st = {}

        def branches():
            pa = jnp.dot(abuf[rows, :], wpa_ref[...], preferred_element_type=F32)
            pb = jnp.dot(obuf[rows, :], wpb_ref[...], preferred_element_type=F32)
            st["merged"] = (ga_ref[rows, :].astype(F32) * pa + gb_ref[rows, :].astype(F32) * pb).astype(BF16)

        def output():
            st["x1"] = x_ref[rows, :] + jnp.dot(st["merged"], wo_ref[...], preferred_element_type=F32)
            x1_ref[rows, :] = st["x1"]

        def router():
            h2 = _rms(st["x1"], ng_ref[...])
            hi = h2.astype(BF16)
            lo = (h2 - hi.astype(F32)).astype(BF16)
            both = jnp.dot(hi, wr_ref[...], preferred_element_type=F32)
            logits = (both[:, :LANES] + jnp.dot(lo, wr_ref[:, :LANES], preferred_element_type=F32)
                      + both[:, LANES:])[:, :N_EXPERTS]
            ex = jnp.exp(logits - jnp.max(logits, axis=-1, keepdims=True))
            aff_ref[rows, :] = ex / jnp.sum(ex, axis=-1, keepdims=True)

        return [branches, output, router]

    group = 2
    ngroups = nblk // group
    grows = group * BLOCK
    proj = [projection_pieces(slice(j * grows, (j + 1) * grows)) for j in range(ngroups)]
    gating = [functools.partial(spatial_gating, c) for c in range(nblk)]
    for _ in range(group):
        gating.pop(0)()
    routers = []
    for j in range(ngroups):
        fill = list(proj[j - 1][:-1]) if j else []
        while len(fill) < 3 and gating:
            fill.append(gating.pop(0))
        for stage in attention_stages(range(j * group, (j + 1) * group)):
            stage()
            if fill:
                fill.pop(0)()
        for f in fill:
            f()
        if j:
            routers.append(proj[j - 1][-1])
    for f in gating:
        f()
    for piece in proj[-1][:-1]:
        piece()
    for router in routers + [proj[-1][-1]]:
        router()


def _mix(B, S, u, vn, q, k, v, ga, gb, xf, wsp, bsp, bias, sink, wpa, wpb, wo, ng, wr2):
    T = B * S
    tq = TQ_MIX
    nq = S // tq
    bpq = tq // BLOCK
    nb = S // BLOCK
    row = lambda w: pl.BlockSpec((tq, w), lambda b, i: (b * nq + i, 0))
    prev = pl.BlockSpec((BLOCK, KV_DUP), lambda b, i: (b * nb + jnp.maximum(i * bpq - 1, 0), 0))
    nxt = pl.BlockSpec((BLOCK, KV_DUP), lambda b, i: (b * nb + jnp.minimum(i * bpq + bpq, nb - 1), 0))
    full = lambda a: pl.BlockSpec(a.shape, lambda b, i: (0,) * a.ndim, pipeline_mode=pl.Buffered(1))
    return pl.pallas_call(
        _mix_kernel,
        grid=(B, nq),
        in_specs=[row(A_WIDTH), row(A_WIDTH), row(B_WIDTH),
                  row(KV_DUP), prev, nxt, row(KV_DUP), prev, nxt,
                  row(D_MODEL), row(D_MODEL), row(D_MODEL),
                  full(wsp), full(bsp), full(bias), full(sink),
                  full(wpa), full(wpb), full(wo), full(ng), full(wr2)],
        out_specs=[row(D_MODEL), row(N_EXPERTS)],
        out_shape=[jax.ShapeDtypeStruct((T, D_MODEL), F32),
                   jax.ShapeDtypeStruct((T, N_EXPERTS), F32)],
        scratch_shapes=[pltpu.VMEM((tq + 2 * WINDOW, KV_DUP), BF16),
                        pltpu.VMEM((tq + 2 * WINDOW, KV_DUP), BF16),
                        pltpu.VMEM((tq, A_WIDTH), BF16),
                        pltpu.VMEM((tq, B_WIDTH), BF16)],
        compiler_params=pltpu.CompilerParams(dimension_semantics=("arbitrary", "arbitrary"),
                                             vmem_limit_bytes=VMEM_LIMIT),
        name="mix",
    )(u, vn, q, k, k, k, v, v, v, ga, gb, xf, wsp, bsp, bias, sink, wpa, wpb, wo, ng, wr2)


def _topk_kernel(afft_ref, tri_ref, idx_ref, gate_ref, cum, cend, *, cap):
    R, S = afft_ref.shape
    nblk = S // LANES
    j = pl.program_id(1)
    lane = lax.broadcasted_iota(jnp.int32, (1, LANES), 1)

    @pl.when(j == 0)
    def _():
        bits = pltpu.bitcast(afft_ref[...], jnp.int32)

        def search(_, carry):
            lo, hi = carry
            mid = lo + ((hi - lo) >> 1)
            cnt = jnp.sum(jnp.where(bits >= mid, 1.0, 0.0), axis=1, keepdims=True)
            ge = cnt >= cap
            return jnp.where(ge, mid, lo), jnp.where(ge, hi, mid)

        lo0 = jnp.zeros((R, 1), jnp.int32)
        hi0 = jnp.full((R, 1), INF_BITS, jnp.int32)
        thr, _ = lax.fori_loop(0, 31, search, (lo0, hi0))

        def prefix(mask):
            outs, carry, ends = [], jnp.zeros((R, 1), F32), jnp.zeros((R, LANES), F32)
            for jj in range(nblk):
                mj = mask[:, jj * LANES:(jj + 1) * LANES]
                outs.append(jnp.dot(mj.astype(BF16), tri_ref[...], preferred_element_type=F32) + carry)
                carry = carry + jnp.sum(mj, axis=1, keepdims=True)
                ends = jnp.where(lane == jj, carry, ends)
            return jnp.concatenate(outs, axis=1), ends

        gt = jnp.where(bits > thr, 1.0, 0.0)
        eq = jnp.where(bits == thr, 1.0, 0.0)
        need = cap - jnp.sum(gt, axis=1, keepdims=True)
        sel = gt + eq * jnp.where(prefix(eq)[0] < need, 1.0, 0.0)
        before, ends = prefix(sel)
        cum[...] = before + sel
        cend[...] = ends

    slot = lax.broadcasted_iota(jnp.int32, (cap, 1), 0).astype(F32)

    def by_block(row):
        return jnp.concatenate([row[:, b * LANES:(b + 1) * LANES] for b in range(nblk)], axis=0)

    def compact(rr):
        r = j * TOPK_ROWS + rr
        done = jnp.where((cend[pl.ds(r, 1), :] <= slot) & (lane < nblk), 1.0, 0.0)
        blk = jnp.sum(done, axis=1, keepdims=True).astype(jnp.int32)
        pick = jnp.where(lane == blk, 1.0, 0.0).astype(BF16)[:, :nblk]
        cum_b = jnp.dot(pick, by_block(cum[pl.ds(r, 1), :]).astype(BF16), preferred_element_type=F32)
        off = jnp.sum(jnp.where(cum_b <= slot, 1.0, 0.0), axis=1, keepdims=True).astype(jnp.int32)
        idx_ref[rr] = blk * LANES + off
        a = by_block(afft_ref[pl.ds(r, 1), :])
        a_hi = a.astype(BF16)
        r1 = a - a_hi.astype(F32)
        a_mid = r1.astype(BF16)
        a_lo = (r1 - a_mid.astype(F32)).astype(BF16)
        aff_b = (jnp.dot(pick, a_hi, preferred_element_type=F32)
                 + jnp.dot(pick, a_mid, preferred_element_type=F32)
                 + jnp.dot(pick, a_lo, preferred_element_type=F32))
        gate_ref[rr] = jnp.sum(jnp.where(lane == off, aff_b, 0.0), axis=1, keepdims=True)

    for rr in range(TOPK_ROWS):
        compact(rr)


def _topk(B, S, cap, afft, tri):
    nb = min(TOPK_BATCHES, B)
    R = nb * N_EXPERTS
    steps = R // TOPK_ROWS
    out_block = pl.BlockSpec((TOPK_ROWS, cap, 1), lambda i, j: (i * steps + j, 0, 0))
    return pl.pallas_call(
        functools.partial(_topk_kernel, cap=cap),
        grid=(B // nb, steps),
        in_specs=[pl.BlockSpec((R, S), lambda i, j: (i, 0)),
                  pl.BlockSpec(tri.shape, lambda i, j: (0, 0))],
        out_specs=[out_block, out_block],
        out_shape=[jax.ShapeDtypeStruct((B * N_EXPERTS, cap, 1), jnp.int32),
                   jax.ShapeDtypeStruct((B * N_EXPERTS, cap, 1), F32)],
        scratch_shapes=[pltpu.VMEM((R, S), F32), pltpu.VMEM((R, LANES), F32)],
        compiler_params=pltpu.CompilerParams(dimension_semantics=("arbitrary", "arbitrary"),
                                             vmem_limit_bytes=VMEM_LIMIT),
        name="topk",
    )(afft, tri)


def _moe_kernel(idx_ref, x1_hbm, gate_ref, wg_ref, wu_ref, wd_ref, ng_ref, out_hbm,
                acc, hbuf, xg, ybuf, sem_in, sem, *, S, cap):
    g = pl.program_id(0)
    e2 = pl.program_id(1)
    k = pl.program_id(2)
    nexp = 2 * pl.num_programs(1)
    half = D_MODEL // 2
    rows0 = g * (MOE_GROUP * S)
    hi_mask = jnp.int32(-65536)

    def lists(seq, ee):
        return ((g * MOE_GROUP + seq) * nexp + ee) * cap

    def gather_jobs(seq, ee, par):
        base = lists(seq, ee)

        def job(c0):
            for c in range(c0, c0 + SCATTER_ROWS):
                r = seq * S + idx_ref[base + c]
                xg[par, seq * cap + c:seq * cap + c + 1, :] = hbuf[pl.ds(r, 1), :]
        return [functools.partial(job, c0) for c0 in range(0, cap, SCATTER_ROWS)]

    def scatter_jobs(seq, ee, par):
        base = lists(seq, ee)

        def job(c0):
            rows = [seq * S + idx_ref[base + c] for c in range(c0, c0 + SCATTER_ROWS)]
            old = [acc[pl.ds(r, 1), :] for r in rows]
            for j, r in enumerate(rows):
                c = seq * cap + c0 + j
                acc[pl.ds(r, 1), :] = old[j] + ybuf[par, c:c + 1, :]
        return [functools.partial(job, c0) for c0 in range(0, cap, SCATTER_ROWS)]

    def ffn(par, accumulate, jobs):
        n_up, n_down = wg_ref.shape[2] // FF_CHUNK, D_MODEL // FF_CHUNK
        unit = -(-len(jobs) // (2 * n_up + n_down - 1))
        jobs = list(jobs)

        def run_share(units):
            for job in jobs[:units * unit]:
                job()
            del jobs[:units * unit]

        p = xg[par]
        lo = pltpu.bitcast(p << 16, F32).astype(BF16)
        hi = pltpu.bitcast(p & hi_mask, F32).astype(BF16)
        x = jnp.concatenate([lo, hi], axis=1)
        hid = []
        for f in range(n_up):
            cols = slice(f * FF_CHUNK, (f + 1) * FF_CHUNK)
            gt = jnp.dot(x, wg_ref[0, :, cols], preferred_element_type=F32)
            up = jnp.dot(x, wu_ref[0, :, cols], preferred_element_type=F32)
            hid.append((gt * jax.nn.sigmoid(gt) * up).astype(BF16))
            run_share(2)
        hid = jnp.concatenate(hid, axis=1)
        gate = jnp.concatenate([gate_ref[s, 0] for s in range(MOE_GROUP)], axis=0)
        for n in range(n_down):
            cols = slice(n * FF_CHUNK, (n + 1) * FF_CHUNK)
            y = jnp.dot(hid, wd_ref[0, :, cols], preferred_element_type=F32) * gate
            ybuf[par, :, cols] = ybuf[par, :, cols] + y if accumulate else y
            run_share(1 if n < n_down - 1 else 0)
        assert not jobs

    @pl.when((e2 == 0) & (k == 0))
    def _():
        def load(i):
            start = i * cap if isinstance(i, int) else pl.multiple_of(i * cap, cap)
            return pltpu.make_async_copy(x1_hbm.at[pl.ds(rows0 + start, cap), :],
                                         acc.at[pl.ds(start, cap), :], sem_in.at[i])

        for i in range(sem_in.shape[0]):
            load(i).start()

        def pack(i, _):
            load(i).wait()
            rows = pl.ds(pl.multiple_of(i * cap, cap), cap)
            hb = _rms(acc[rows, :], ng_ref[...]).astype(BF16).astype(F32)
            u = pltpu.bitcast(hb, jnp.int32)
            hbuf[rows, :] = lax.shift_right_logical(u[:, :half], 16) | (u[:, half:] & hi_mask)
            return 0
        lax.fori_loop(0, MOE_GROUP * S // cap, pack, 0)
        ybuf[1] = jnp.zeros(ybuf.shape[1:], F32)
        for seq in range(MOE_GROUP):
            for job in gather_jobs(seq, 0, 0):
                job()

    for kk in range(4):
        @pl.when(k == kk)
        def _(par=kk // 2, fh=kk % 2):
            e = 2 * e2 + par
            sc = scatter_jobs(fh, jnp.maximum(e - 1, 0), 1 - par)
            ga = gather_jobs(fh, jnp.minimum(e + 1, nexp - 1), 1 - par)
            ffn(par, fh == 1, [job for pair in zip(sc, ga) for job in pair])

    @pl.when((e2 == pl.num_programs(1) - 1) & (k == 3))
    def _():
        for seq in range(MOE_GROUP):
            for job in scatter_jobs(seq, nexp - 1, 1):
                job()
        cp = pltpu.make_async_copy(acc, out_hbm.at[pl.ds(rows0, MOE_GROUP * S), :], sem)
        cp.start()
        cp.wait()


def _moe(B, S, cap, idx_flat, x1, gates, wg, wu, wd, ng):
    G = MOE_GROUP
    T = B * S
    fhw = EXPERT_FF // 2
    ex = lambda e2, k: 2 * e2 + k // 2
    fh = lambda k: k % 2
    grid_spec = pltpu.PrefetchScalarGridSpec(
        num_scalar_prefetch=1,
        grid=(B // G, N_EXPERTS // 2, 4),
        in_specs=[pl.BlockSpec(memory_space=pl.ANY),
                  pl.BlockSpec((G, 1, cap, 1), lambda g, e2, k, idx: (g, ex(e2, k), 0, 0)),
                  pl.BlockSpec((1, D_MODEL, fhw), lambda g, e2, k, idx: (ex(e2, k), 0, fh(k))),
                  pl.BlockSpec((1, D_MODEL, fhw), lambda g, e2, k, idx: (ex(e2, k), 0, fh(k))),
                  pl.BlockSpec((1, fhw, D_MODEL), lambda g, e2, k, idx: (ex(e2, k), fh(k), 0)),
                  pl.BlockSpec((1, D_MODEL), lambda g, e2, k, idx: (0, 0))],
        out_specs=pl.BlockSpec(memory_space=pl.ANY),
        scratch_shapes=[pltpu.VMEM((G * S, D_MODEL), F32),
                        pltpu.VMEM((G * S, D_MODEL // 2), jnp.int32),
                        pltpu.VMEM((2, G * cap, D_MODEL // 2), jnp.int32),
                        pltpu.VMEM((2, G * cap, D_MODEL), F32),
                        pltpu.SemaphoreType.DMA((G * S // cap,)),
                        pltpu.SemaphoreType.DMA(())],
    )
    return pl.pallas_call(
        functools.partial(_moe_kernel, S=S, cap=cap),
        grid_spec=grid_spec,
        out_shape=jax.ShapeDtypeStruct((T, D_MODEL), F32),
        input_output_aliases={1: 0},
        compiler_params=pltpu.CompilerParams(dimension_semantics=("arbitrary",) * 3,
                                             vmem_limit_bytes=VMEM_LIMIT),
        name="moe",
    )(idx_flat, x1, gates, wg, wu, wd, ng)


def _t5_bucket(rel):
    nb = N_BUCKETS // 2
    max_exact = nb // 2
    ret = (rel > 0).astype(np.int32) * nb
    n = np.abs(rel)
    large = max_exact + (np.log(np.maximum(n, 1) / max_exact) / np.log(MAX_DISTANCE / max_exact)
                         * (nb - max_exact)).astype(np.int32)
    large = np.minimum(large, nb - 1)
    return (ret + np.where(n < max_exact, n, large)).astype(np.int32)


def _bias_kernel(table_ref, bucket_ref, out_ref):
    bucket = bucket_ref[...]
    row = lax.broadcasted_iota(jnp.int32, (BLOCK, SPAN), 0)
    col = lax.broadcasted_iota(jnp.int32, (BLOCK, SPAN), 1)
    band = jnp.abs(col - WINDOW - row) <= WINDOW
    valid = (band & (col >= WINDOW), band, band & (col < WINDOW + BLOCK))
    for h in range(N_HEADS):
        b = jnp.zeros((BLOCK, SPAN), F32)
        for n in range(N_BUCKETS):
            b = jnp.where(bucket == n, table_ref[n, h], b)
        kh, slab, half = h // Q_GROUP, (h % Q_GROUP) // 2, h % 2
        for var in range(3):
            out_ref[var, kh, slab * BLOCK:(slab + 1) * BLOCK, half * SPAN:(half + 1) * SPAN] = (
                jnp.where(valid[var], b, -jnp.inf))


def _attention_bias(rel_table):
    rel = (np.arange(SPAN)[None, :] - WINDOW) - np.arange(BLOCK)[:, None]
    bucket = jnp.asarray(_t5_bucket(rel))
    return pl.pallas_call(
        _bias_kernel,
        in_specs=[pl.BlockSpec(memory_space=pltpu.SMEM),
                  pl.BlockSpec(bucket.shape, lambda: (0, 0))],
        out_specs=pl.BlockSpec((3, N_KV_HEADS, 2 * BLOCK, 2 * SPAN), lambda: (0, 0, 0, 0)),
        out_shape=jax.ShapeDtypeStruct((3, N_KV_HEADS, 2 * BLOCK, 2 * SPAN), F32),
        name="t5bias",
    )(rel_table.astype(F32), bucket)


def _dup_heads(w):
    parts = []
    for h in range(N_KV_HEADS):
        wh = w[..., h * HEAD_DIM:(h + 1) * HEAD_DIM]
        parts += [wh, wh]
    return jnp.concatenate(parts, axis=-1)


def _mean_matrix(width):
    blk = np.arange(width) // HEAD_DIM
    return jnp.asarray((blk[:, None] == blk[None, :]).astype(np.float32) / HEAD_DIM, dtype=BF16)


def kernel(x, norm_mix_g, w_in, b_gate, vnorm_g, w_spatial, b_spatial, q_norm_g, k_norm_g,
           attn_sink, rel_bias_table, w_proj_a, w_proj_b, w_out, norm_ffn_g, w_router,
           w_gate_e, w_up_e, w_down_e):
    B, S, _ = x.shape
    T = B * S
    cap = CAPACITY_FACTOR * S // N_EXPERTS
    assert S % TQ_MIX == 0 and T % TM_INPROJ == 0 and B % MOE_GROUP == 0
    assert B % min(TOPK_BATCHES, B) == 0
    xf = x.reshape(T, D_MODEL)
    l = 0

    w = w_in[l]
    o_q = 2 * A_WIDTH
    o_k = o_q + B_WIDTH
    o_v = o_k + KV_WIDTH
    o_g = o_v + KV_WIDTH
    w_all = jnp.concatenate([w[:, :o_k], _dup_heads(w[:, o_k:o_v]), _dup_heads(w[:, o_v:o_g]),
                             w[:, o_g:]], axis=1).astype(BF16)
    qg = jnp.tile(q_norm_g[l], N_HEADS)[None, :] * (HEAD_DIM ** -0.5)
    kg = jnp.tile(k_norm_g[l], 2 * N_KV_HEADS)[None, :]
    wsp = w_spatial[l].reshape(A_GROUPS // 2, 2, CHUNK, CHUNK).transpose(0, 2, 1, 3)
    wsp = wsp.reshape(A_GROUPS // 2, CHUNK, 2 * CHUNK).astype(BF16)
    bsp = jnp.repeat(b_spatial[l].T, A_GROUP_DIM, axis=1)
    bias = _attention_bias(rel_bias_table)
    sink = attn_sink[l].astype(F32).reshape(N_KV_HEADS, 2, 2)
    sink = jnp.broadcast_to(sink.transpose(0, 2, 1)[:, :, :, None, None],
                            (N_KV_HEADS, 2, 2, BLOCK, 1)).reshape(N_KV_HEADS, 2, 2 * BLOCK, 1)
    wr = w_router[l]
    wrh = wr.astype(BF16)
    wrl = (wr - wrh.astype(F32)).astype(BF16)
    pad = jnp.zeros((D_MODEL, LANES - N_EXPERTS), BF16)
    wr2 = jnp.concatenate([wrh, pad, wrl, pad], axis=1)
    tri = jnp.asarray(np.triu(np.ones((LANES, LANES), np.float32), k=1), dtype=BF16)

    (u, vn, q, k, v, ga, gb), (wg, wu, wd) = _inproj(
        xf, norm_mix_g[l][None, :], w_all, b_gate[l][None, :], vnorm_g[l][None, :], qg, kg,
        _mean_matrix(B_WIDTH), _mean_matrix(KV_DUP), (w_gate_e[l], w_up_e[l], w_down_e[l]))
    x1, aff = _mix(B, S, u, vn, q, k, v, ga, gb, xf, wsp, bsp, bias, sink,
                   w_proj_a[l].astype(BF16), w_proj_b[l].astype(BF16), w_out[l].astype(BF16),
                   norm_ffn_g[l][None, :], wr2)
    afft = aff.reshape(B, S, N_EXPERTS).transpose(0, 2, 1).reshape(B * N_EXPERTS, S)
    idx, gates = _topk(B, S, cap, afft, tri)
    out = _moe(B, S, cap, idx.reshape(-1), x1, gates.reshape(B, N_EXPERTS, cap, 1), wg, wu, wd,
               norm_ffn_g[l][None, :])
    return out.reshape(B, S, D_MODEL)
```

```python
import functools

import numpy as np
import jax
import jax.numpy as jnp
from jax import lax
from jax.experimental import pallas as pl
from jax.experimental.pallas import tpu as pltpu

F32 = jnp.float32
BF16 = jnp.bfloat16

D_MODEL = 1024
A_GROUPS = 8
A_GROUP_DIM = 64
A_WIDTH = A_GROUPS * A_GROUP_DIM
CHUNK = 128
N_HEADS = 8
N_KV_HEADS = 2
HEAD_DIM = 64
Q_GROUP = N_HEADS // N_KV_HEADS
B_WIDTH = N_HEADS * HEAD_DIM
KV_WIDTH = N_KV_HEADS * HEAD_DIM
WINDOW = 128
BLOCK = 128
SPAN = BLOCK + 2 * WINDOW
N_BUCKETS = 32
MAX_DISTANCE = 128
N_EXPERTS = 16
EXPERT_FF = 2048
CAPACITY_FACTOR = 2
EPS = 1e-6

LANES = 128
KV_DUP = 2 * KV_WIDTH
TM_INPROJ = 512
TQ_MIX = 1024
STAGE_BLOCKS = 2
FF_CHUNK = 256
MOE_GROUP = 2
SCATTER_ROWS = 8
TOPK_BATCHES = 8
TOPK_ROWS = 8
VMEM_LIMIT = 56 * 1024 * 1024
INF_BITS = 0x7F800000


def _rms(x, g):
    return x * lax.rsqrt(jnp.mean(x * x, axis=-1, keepdims=True) + EPS) * g


def _inproj_kernel(x_ref, g_ref, w_ref, wkv_ref, bg_ref, vg_ref, qg_ref, kg_ref, bdq_ref, bdk_ref,
                   ewg_ref, ewu_ref, ewd_ref,
                   u_ref, vn_ref, q_ref, k_ref, v_ref, ga_ref, gb_ref, bwg_ref, bwu_ref, bwd_ref):
    bwg_ref[...] = ewg_ref[...].astype(BF16)
    bwu_ref[...] = ewu_ref[...].astype(BF16)
    bwd_ref[...] = ewd_ref[...].astype(BF16)

    hb = _rms(x_ref[...], g_ref[...]).astype(BF16)

    def seg(lo, hi):
        return jnp.dot(hb, w_ref[:, lo:hi], preferred_element_type=F32)

    o_va, o_q, o_k = A_WIDTH, 2 * A_WIDTH, 2 * A_WIDTH + B_WIDTH
    o_ga = o_k + 2 * KV_WIDTH
    o_gb = o_ga + D_MODEL
    ga_ref[...] = jax.nn.sigmoid(seg(o_ga, o_gb) + bg_ref[:, :D_MODEL]).astype(BF16)
    gb_ref[...] = jax.nn.sigmoid(seg(o_gb, o_gb + D_MODEL) + bg_ref[:, D_MODEL:]).astype(BF16)
    zq = seg(o_q, o_k)
    msq = jnp.dot((zq * zq).astype(BF16), bdq_ref[...], preferred_element_type=F32)
    q_ref[...] = (zq * lax.rsqrt(msq + EPS) * qg_ref[...]).astype(BF16)
    u_ref[...] = jax.nn.gelu(seg(0, o_va)).astype(BF16)
    gv = jax.nn.gelu(seg(o_va, o_q))
    vn_ref[...] = _rms(gv, vg_ref[...]).astype(BF16)
    zkv = jnp.dot(hb, wkv_ref[...], preferred_element_type=F32)
    zk = zkv[:, :KV_DUP]
    msk = jnp.dot((zk * zk).astype(BF16), bdk_ref[...], preferred_element_type=F32)
    k_ref[...] = (zk * lax.rsqrt(msk + EPS) * kg_ref[...]).astype(BF16)
    v_ref[...] = zkv[:, KV_DUP:].astype(BF16)


def _inproj(xf, norm_g, w_all, w_kv, b_gate, vnorm_g, qg, kg, bdq, bdk, expert_w):
    T = xf.shape[0]
    tm = TM_INPROJ
    steps = T // tm
    row = lambda w: pl.BlockSpec((tm, w), lambda i: (i, 0))
    full = lambda a: pl.BlockSpec(a.shape, lambda i: (0,) * a.ndim)
    out_w = (A_WIDTH, A_WIDTH, B_WIDTH, KV_DUP, KV_DUP, D_MODEL, D_MODEL)
    flat = [w.reshape(-1, w.shape[-1]) for w in expert_w]
    assert all(w.shape[0] % (steps * 16) == 0 for w in flat)
    slab = lambda w: pl.BlockSpec((w.shape[0] // steps, w.shape[1]), lambda i: (i, 0))
    outs = pl.pallas_call(
        _inproj_kernel,
        grid=(steps,),
        in_specs=[row(D_MODEL), full(norm_g), full(w_all), full(w_kv),
                  full(b_gate), full(vnorm_g), full(qg), full(kg), full(bdq), full(bdk)]
                 + [slab(w) for w in flat],
        out_specs=[row(w) for w in out_w] + [slab(w) for w in flat],
        out_shape=[jax.ShapeDtypeStruct((T, w), BF16) for w in out_w]
                  + [jax.ShapeDtypeStruct(w.shape, BF16) for w in flat],
        compiler_params=pltpu.CompilerParams(dimension_semantics=("arbitrary",),
                                             vmem_limit_bytes=VMEM_LIMIT),
        name="inproj",
    )(xf, norm_g, w_all, w_kv, b_gate, vnorm_g, qg, kg, bdq, bdk, *flat)
    return outs[:len(out_w)], [o.reshape(w.shape) for o, w in zip(outs[len(out_w):], expert_w)]


def _block_diag_pair(slab, lo_mask):
    zero = jnp.zeros_like(slab)
    return jnp.concatenate([jnp.where(lo_mask, slab, zero), jnp.where(lo_mask, zero, slab)], axis=0)


def _mix_kernel(u_ref, vn_ref, q_ref, kc_ref, kp_ref, kn_ref, vc_ref, vp_ref, vx_ref,
                ga_ref, gb_ref, x_ref, wsp_ref, bsp_ref, bias_ref, sink_ref,
                wpa_ref, wpb_ref, wo_ref, ng_ref, wr_ref,
                x1_ref, aff_ref, kbuf, vbuf, abuf, obuf):
    tq = x_ref.shape[0]
    nblk = tq // BLOCK
    i = pl.program_id(1)
    last = pl.num_programs(1) - 1
    lo_mask = lax.broadcasted_iota(jnp.int32, (1, LANES), 1) < HEAD_DIM

    kbuf[0:WINDOW, :] = kp_ref[...]
    kbuf[WINDOW:WINDOW + tq, :] = kc_ref[...]
    kbuf[WINDOW + tq:, :] = kn_ref[...]
    vbuf[0:WINDOW, :] = vp_ref[...]
    vbuf[WINDOW:WINDOW + tq, :] = vc_ref[...]
    vbuf[WINDOW + tq:, :] = vx_ref[...]

    def spatial_gating(c):
        rows = slice(c * CHUNK, (c + 1) * CHUNK)
        parts = []
        for j in range(A_GROUPS // 2):
            bd = _block_diag_pair(vn_ref[rows, j * LANES:(j + 1) * LANES], lo_mask)
            parts.append(jnp.dot(wsp_ref[j], bd, preferred_element_type=F32))
        mixed = jnp.concatenate(parts, axis=1) + bsp_ref[...]
        abuf[rows, :] = (u_ref[rows, :].astype(F32) * mixed).astype(BF16)

    vrow = lax.broadcasted_iota(jnp.int32, (2 * SPAN, LANES), 0)
    vcol = lax.broadcasted_iota(jnp.int32, (2 * SPAN, LANES), 1)
    ones_cols = jnp.where((vrow < SPAN) == (vcol < HEAD_DIM), 1.0, 0.0).astype(BF16)

    def attention_stages(blocks):
        units = [(n, kh) for n in blocks for kh in range(N_KV_HEADS)]
        st = {}

        def geometry(n, kh):
            if n == 0:
                var = jnp.where(i == 0, 0, 1)
            elif n == nblk - 1:
                var = jnp.where(i == last, 2, 1)
            else:
                var = 1
            return (slice(n * BLOCK, (n + 1) * BLOCK), slice(n * BLOCK, n * BLOCK + SPAN),
                    slice(kh * LANES, (kh + 1) * LANES), var)

        def scores():
            for n, kh in units:
                rows, span, ksl, var = geometry(n, kh)
                kk = _block_diag_pair(kbuf[span, ksl], lo_mask)
                q2 = jnp.concatenate([q_ref[rows, (2 * kh) * LANES:(2 * kh + 1) * LANES],
                                      q_ref[rows, (2 * kh + 1) * LANES:(2 * kh + 2) * LANES]], axis=0)
                st["s", n, kh] = (lax.dot_general(q2, kk, (((1,), (1,)), ((), ())),
                                                  preferred_element_type=F32)
                                  + bias_ref[var, kh])

        def numerators():
            for n, kh in units:
                p2, m2 = [], []
                for half in range(2):
                    sh = st["s", n, kh][:, half * SPAN:(half + 1) * SPAN]
                    m = jnp.maximum(jnp.max(sh, axis=-1, keepdims=True), sink_ref[kh, half])
                    p2.append(jnp.exp(sh - m).astype(BF16))
                    m2.append(m)
                st["p", n, kh] = jnp.concatenate(p2, axis=1)
                st["m", n, kh] = m2

        def values():
            for n, kh in units:
                rows, span, ksl, _ = geometry(n, kh)
                vv = jnp.concatenate([_block_diag_pair(vbuf[span, ksl], lo_mask), ones_cols], axis=1)
                o2 = jnp.dot(st["p", n, kh], vv, preferred_element_type=F32)
                m2 = st["m", n, kh]
                sink_term = jnp.where(lo_mask, jnp.exp(sink_ref[kh, 0] - m2[0]),
                                      jnp.exp(sink_ref[kh, 1] - m2[1]))
                o = o2[:, :LANES] * (1.0 / (o2[:, LANES:] + sink_term))
                obuf[rows, (2 * kh) * LANES:(2 * kh + 1) * LANES] = o[:BLOCK].astype(BF16)
                obuf[rows, (2 * kh + 1) * LANES:(2 * kh + 2) * LANES] = o[BLOCK:].astype(BF16)

        return [scores, numerators, values]

    def projection_pieces(rows):
        st = {}

        def branches():
            pa = jnp.dot(abuf[rows, :], wpa_ref[...], preferred_element_type=F32)
            pb = jnp.dot(obuf[rows, :], wpb_ref[...], preferred_element_type=F32)
            st["merged"] = (ga_ref[rows, :].astype(F32) * pa + gb_ref[rows, :].astype(F32) * pb).astype(BF16)

        def output():
            st["x1"] = x_ref[rows, :] + jnp.dot(st["merged"], wo_ref[...], preferred_element_type=F32)
            x1_ref[rows, :] = st["x1"]

        def router():
            h2 = _rms(st["x1"], ng_ref[...])
            hi = h2.astype(BF16)
            lo = (h2 - hi.astype(F32)).astype(BF16)
            both = jnp.dot(hi, wr_ref[...], preferred_element_type=F32)
            logits = (both[:, :LANES] + jnp.dot(lo, wr_ref[:, :LANES], preferred_element_type=F32)
                      + both[:, LANES:])[:, :N_EXPERTS]
            ex = jnp.exp(logits - jnp.max(logits, axis=-1, keepdims=True))
            aff_ref[rows, :] = ex / jnp.sum(ex, axis=-1, keepdims=True)

        return [branches, output, router]

    group = STAGE_BLOCKS
    ngroups = nblk // group
    grows = group * BLOCK
    proj = [projection_pieces(slice(j * grows, (j + 1) * grows)) for j in range(ngroups)]
    gating = [functools.partial(spatial_gating, c) for c in range(nblk)]
    for _ in range(group):
        gating.pop(0)()
    routers = []
    for j in range(ngroups):
        fill = list(proj[j - 1][:-1]) if j else []
        while len(fill) < 3 and gating:
            fill.append(gating.pop(0))
        for stage in attention_stages(range(j * group, (j + 1) * group)):
            stage()
            if fill:
                fill.pop(0)()
        for f in fill:
            f()
        if j:
            routers.append(proj[j - 1][-1])
    for f in gating:
        f()
    for piece in proj[-1][:-1]:
        piece()
    for router in routers + [proj[-1][-1]]:
        router()


def _mix(B, S, u, vn, q, k, v, ga, gb, xf, wsp, bsp, bias, sink, wpa, wpb, wo, ng, wr2):
    T = B * S
    tq = TQ_MIX
    nq = S // tq
    bpq = tq // BLOCK
    nb = S // BLOCK
    row = lambda w: pl.BlockSpec((tq, w), lambda b, i: (b * nq + i, 0))
    prev = pl.BlockSpec((BLOCK, KV_DUP), lambda b, i: (b * nb + jnp.maximum(i * bpq - 1, 0), 0))
    nxt = pl.BlockSpec((BLOCK, KV_DUP), lambda b, i: (b * nb + jnp.minimum(i * bpq + bpq, nb - 1), 0))
    full = lambda a: pl.BlockSpec(a.shape, lambda b, i: (0,) * a.ndim, pipeline_mode=pl.Buffered(1))
    return pl.pallas_call(
        _mix_kernel,
        grid=(B, nq),
        in_specs=[row(A_WIDTH), row(A_WIDTH), row(B_WIDTH),
                  row(KV_DUP), prev, nxt, row(KV_DUP), prev, nxt,
                  row(D_MODEL), row(D_MODEL), row(D_MODEL),
                  full(wsp), full(bsp), full(bias), full(sink),
                  full(wpa), full(wpb), full(wo), full(ng), full(wr2)],
        out_specs=[row(D_MODEL), row(N_EXPERTS)],
        out_shape=[jax.ShapeDtypeStruct((T, D_MODEL), F32),
                   jax.ShapeDtypeStruct((T, N_EXPERTS), F32)],
        scratch_shapes=[pltpu.VMEM((tq + 2 * WINDOW, KV_DUP), BF16),
                        pltpu.VMEM((tq + 2 * WINDOW, KV_DUP), BF16),
                        pltpu.VMEM((tq, A_WIDTH), BF16),
                        pltpu.VMEM((tq, B_WIDTH), BF16)],
        compiler_params=pltpu.CompilerParams(dimension_semantics=("arbitrary", "arbitrary"),
                                             vmem_limit_bytes=VMEM_LIMIT),
        name="mix",
    )(u, vn, q, k, k, k, v, v, v, ga, gb, xf, wsp, bsp, bias, sink, wpa, wpb, wo, ng, wr2)


def _topk_kernel(afft_ref, tri_ref, idx_ref, gate_ref, cum, cend, *, cap):
    R, S = afft_ref.shape
    nblk = S // LANES
    j = pl.program_id(1)
    lane = lax.broadcasted_iota(jnp.int32, (1, LANES), 1)

    @pl.when(j == 0)
    def _():
        bits = pltpu.bitcast(afft_ref[...], jnp.int32)

        def search(_, carry):
            lo, hi = carry
            mid = lo + ((hi - lo) >> 1)
            cnt = jnp.sum(jnp.where(bits >= mid, 1.0, 0.0), axis=1, keepdims=True)
            ge = cnt >= cap
            return jnp.where(ge, mid, lo), jnp.where(ge, hi, mid)

        lo0 = jnp.zeros((R, 1), jnp.int32)
        hi0 = jnp.full((R, 1), INF_BITS, jnp.int32)
        thr, _ = lax.fori_loop(0, 31, search, (lo0, hi0))

        def prefix(mask):
            outs, carry, ends = [], jnp.zeros((R, 1), F32), jnp.zeros((R, LANES), F32)
            for jj in range(nblk):
                mj = mask[:, jj * LANES:(jj + 1) * LANES]
                outs.append(jnp.dot(mj.astype(BF16), tri_ref[...], preferred_element_type=F32) + carry)
                carry = carry + jnp.sum(mj, axis=1, keepdims=True)
                ends = jnp.where(lane == jj, carry, ends)
            return jnp.concatenate(outs, axis=1), ends

        gt = jnp.where(bits > thr, 1.0, 0.0)
        eq = jnp.where(bits == thr, 1.0, 0.0)
        need = cap - jnp.sum(gt, axis=1, keepdims=True)
        sel = gt + eq * jnp.where(prefix(eq)[0] < need, 1.0, 0.0)
        before, ends = prefix(sel)
        cum[...] = before + sel
        cend[...] = ends

    slot = lax.broadcasted_iota(jnp.int32, (cap, 1), 0).astype(F32)

    def by_block(row):
        return jnp.concatenate([row[:, b * LANES:(b + 1) * LANES] for b in range(nblk)], axis=0)

    def compact(rr):
        r = j * TOPK_ROWS + rr
        done = jnp.where((cend[pl.ds(r, 1), :] <= slot) & (lane < nblk), 1.0, 0.0)
        blk = jnp.sum(done, axis=1, keepdims=True).astype(jnp.int32)
        pick = jnp.where(lane == blk, 1.0, 0.0).astype(BF16)[:, :nblk]
        cum_b = jnp.dot(pick, by_block(cum[pl.ds(r, 1), :]).astype(BF16), preferred_element_type=F32)
        off = jnp.sum(jnp.where(cum_b <= slot, 1.0, 0.0), axis=1, keepdims=True).astype(jnp.int32)
        idx_ref[rr] = blk * LANES + off
        a = by_block(afft_ref[pl.ds(r, 1), :])
        a_hi = a.astype(BF16)
        r1 = a - a_hi.astype(F32)
        a_mid = r1.astype(BF16)
        a_lo = (r1 - a_mid.astype(F32)).astype(BF16)
        aff_b = (jnp.dot(pick, a_hi, preferred_element_type=F32)
                 + jnp.dot(pick, a_mid, preferred_element_type=F32)
                 + jnp.dot(pick, a_lo, preferred_element_type=F32))
        gate_ref[rr] = jnp.sum(jnp.where(lane == off, aff_b, 0.0), axis=1, keepdims=True)

    for rr in range(TOPK_ROWS):
        compact(rr)


def _topk(B, S, cap, afft, tri):
    nb = min(TOPK_BATCHES, B)
    R = nb * N_EXPERTS
    steps = R // TOPK_ROWS
    out_block = pl.BlockSpec((TOPK_ROWS, cap, 1), lambda i, j: (i * steps + j, 0, 0))
    return pl.pallas_call(
        functools.partial(_topk_kernel, cap=cap),
        grid=(B // nb, steps),
        in_specs=[pl.BlockSpec((R, S), lambda i, j: (i, 0)),
                  pl.BlockSpec(tri.shape, lambda i, j: (0, 0))],
        out_specs=[out_block, out_block],
        out_shape=[jax.ShapeDtypeStruct((B * N_EXPERTS, cap, 1), jnp.int32),
                   jax.ShapeDtypeStruct((B * N_EXPERTS, cap, 1), F32)],
        scratch_shapes=[pltpu.VMEM((R, S), F32), pltpu.VMEM((R, LANES), F32)],
        compiler_params=pltpu.CompilerParams(dimension_semantics=("arbitrary", "arbitrary"),
                                             vmem_limit_bytes=VMEM_LIMIT),
        name="topk",
    )(afft, tri)


def _moe_kernel(idx_ref, x1_hbm, gate_ref, wg_ref, wu_ref, wd_ref, ng_ref, out_hbm,
                acc, hbuf, xg, ybuf, sem_in, sem, *, S, cap):
    g = pl.program_id(0)
    e2 = pl.program_id(1)
    k = pl.program_id(2)
    nexp = 2 * pl.num_programs(1)
    half = D_MODEL // 2
    rows0 = g * (MOE_GROUP * S)
    hi_mask = jnp.int32(-65536)

    def lists(seq, ee):
        return ((g * MOE_GROUP + seq) * nexp + ee) * cap

    def gather_jobs(seq, ee, par):
        base = lists(seq, ee)

        def job(c0):
            for c in range(c0, c0 + SCATTER_ROWS):
                r = seq * S + idx_ref[base + c]
                xg[par, seq * cap + c:seq * cap + c + 1, :] = hbuf[pl.ds(r, 1), :]
        return [functools.partial(job, c0) for c0 in range(0, cap, SCATTER_ROWS)]

    def scatter_jobs(seq, ee, par):
        base = lists(seq, ee)

        def job(c0):
            rows = [seq * S + idx_ref[base + c] for c in range(c0, c0 + SCATTER_ROWS)]
            old = [acc[pl.ds(r, 1), :] for r in rows]
            for j, r in enumerate(rows):
                c = seq * cap + c0 + j
                acc[pl.ds(r, 1), :] = old[j] + ybuf[par, c:c + 1, :]
        return [functools.partial(job, c0) for c0 in range(0, cap, SCATTER_ROWS)]

    def ffn(par, accumulate, jobs):
        n_up, n_down = wg_ref.shape[2] // FF_CHUNK, D_MODEL // FF_CHUNK
        unit = -(-len(jobs) // (2 * n_up + n_down - 1))
        jobs = list(jobs)

        def run_share(units):
            for job in jobs[:units * unit]:
                job()
            del jobs[:units * unit]

        p = xg[par]
        lo = pltpu.bitcast(p << 16, F32).astype(BF16)
        hi = pltpu.bitcast(p & hi_mask, F32).astype(BF16)
        x = jnp.concatenate([lo, hi], axis=1)
        hid = []
        for f in range(n_up):
            cols = slice(f * FF_CHUNK, (f + 1) * FF_CHUNK)
            gt = jnp.dot(x, wg_ref[0, :, cols], preferred_element_type=F32)
            up = jnp.dot(x, wu_ref[0, :, cols], preferred_element_type=F32)
            hid.append((gt * jax.nn.sigmoid(gt) * up).astype(BF16))
            run_share(2)
        hid = jnp.concatenate(hid, axis=1)
        gate = jnp.concatenate([gate_ref[s, 0] for s in range(MOE_GROUP)], axis=0)
        for n in range(n_down):
            cols = slice(n * FF_CHUNK, (n + 1) * FF_CHUNK)
            y = jnp.dot(hid, wd_ref[0, :, cols], preferred_element_type=F32) * gate
            ybuf[par, :, cols] = ybuf[par, :, cols] + y if accumulate else y
            run_share(1 if n < n_down - 1 else 0)
        assert not jobs

    @pl.when((e2 == 0) & (k == 0))
    def _():
        def load(i):
            start = i * cap if isinstance(i, int) else pl.multiple_of(i * cap, cap)
            return pltpu.make_async_copy(x1_hbm.at[pl.ds(rows0 + start, cap), :],
                                         acc.at[pl.ds(start, cap), :], sem_in.at[i])

        for i in range(sem_in.shape[0]):
            load(i).start()

        def pack(i, _):
            load(i).wait()
            rows = pl.ds(pl.multiple_of(i * cap, cap), cap)
            hb = _rms(acc[rows, :], ng_ref[...]).astype(BF16).astype(F32)
            u = pltpu.bitcast(hb, jnp.int32)
            hbuf[rows, :] = lax.shift_right_logical(u[:, :half], 16) | (u[:, half:] & hi_mask)
            return 0
        lax.fori_loop(0, MOE_GROUP * S // cap, pack, 0)
        ybuf[1] = jnp.zeros(ybuf.shape[1:], F32)
        for seq in range(MOE_GROUP):
            for job in gather_jobs(seq, 0, 0):
                job()

    for kk in range(4):
        @pl.when(k == kk)
        def _(par=kk // 2, fh=kk % 2):
            e = 2 * e2 + par
            sc = scatter_jobs(fh, jnp.maximum(e - 1, 0), 1 - par)
            ga = gather_jobs(fh, jnp.minimum(e + 1, nexp - 1), 1 - par)
            ffn(par, fh == 1, [job for pair in zip(sc, ga) for job in pair])

    @pl.when((e2 == pl.num_programs(1) - 1) & (k == 3))
    def _():
        for seq in range(MOE_GROUP):
            for job in scatter_jobs(seq, nexp - 1, 1):
                job()
        cp = pltpu.make_async_copy(acc, out_hbm.at[pl.ds(rows0, MOE_GROUP * S), :], sem)
        cp.start()
        cp.wait()


def _moe(B, S, cap, idx_flat, x1, gates, wg, wu, wd, ng):
    G = MOE_GROUP
    T = B * S
    fhw = EXPERT_FF // 2
    ex = lambda e2, k: 2 * e2 + k // 2
    fh = lambda k: k % 2
    grid_spec = pltpu.PrefetchScalarGridSpec(
        num_scalar_prefetch=1,
        grid=(B // G, N_EXPERTS // 2, 4),
        in_specs=[pl.BlockSpec(memory_space=pl.ANY),
                  pl.BlockSpec((G, 1, cap, 1), lambda g, e2, k, idx: (g, ex(e2, k), 0, 0)),
                  pl.BlockSpec((1, D_MODEL, fhw), lambda g, e2, k, idx: (ex(e2, k), 0, fh(k))),
                  pl.BlockSpec((1, D_MODEL, fhw), lambda g, e2, k, idx: (ex(e2, k), 0, fh(k))),
                  pl.BlockSpec((1, fhw, D_MODEL), lambda g, e2, k, idx: (ex(e2, k), fh(k), 0)),
                  pl.BlockSpec((1, D_MODEL), lambda g, e2, k, idx: (0, 0))],
        out_specs=pl.BlockSpec(memory_space=pl.ANY),
        scratch_shapes=[pltpu.VMEM((G * S, D_MODEL), F32),
                        pltpu.VMEM((G * S, D_MODEL // 2), jnp.int32),
                        pltpu.VMEM((2, G * cap, D_MODEL // 2), jnp.int32),
                        pltpu.VMEM((2, G * cap, D_MODEL), F32),
                        pltpu.SemaphoreType.DMA((G * S // cap,)),
                        pltpu.SemaphoreType.DMA(())],
    )
    return pl.pallas_call(
        functools.partial(_moe_kernel, S=S, cap=cap),
        grid_spec=grid_spec,
        out_shape=jax.ShapeDtypeStruct((T, D_MODEL), F32),
        input_output_aliases={1: 0},
        compiler_params=pltpu.CompilerParams(dimension_semantics=("arbitrary",) * 3,
                                             vmem_limit_bytes=VMEM_LIMIT),
        name="moe",
    )(idx_flat, x1, gates, wg, wu, wd, ng)


def _t5_bucket(rel):
    nb = N_BUCKETS // 2
    max_exact = nb // 2
    ret = (rel > 0).astype(np.int32) * nb
    n = np.abs(rel)
    large = max_exact + (np.log(np.maximum(n, 1) / max_exact) / np.log(MAX_DISTANCE / max_exact)
                         * (nb - max_exact)).astype(np.int32)
    large = np.minimum(large, nb - 1)
    return (ret + np.where(n < max_exact, n, large)).astype(np.int32)


def _bias_kernel(table_ref, bucket_ref, out_ref):
    bucket = bucket_ref[...]
    row = lax.broadcasted_iota(jnp.int32, (BLOCK, SPAN), 0)
    col = lax.broadcasted_iota(jnp.int32, (BLOCK, SPAN), 1)
    band = jnp.abs(col - WINDOW - row) <= WINDOW
    valid = (band & (col >= WINDOW), band, band & (col < WINDOW + BLOCK))
    for h in range(N_HEADS):
        b = jnp.zeros((BLOCK, SPAN), F32)
        for n in range(N_BUCKETS):
            b = jnp.where(bucket == n, table_ref[n, h], b)
        kh, slab, half = h // Q_GROUP, (h % Q_GROUP) // 2, h % 2
        for var in range(3):
            out_ref[var, kh, slab * BLOCK:(slab + 1) * BLOCK, half * SPAN:(half + 1) * SPAN] = (
                jnp.where(valid[var], b, -jnp.inf))


def _attention_bias(rel_table):
    rel = (np.arange(SPAN)[None, :] - WINDOW) - np.arange(BLOCK)[:, None]
    bucket = jnp.asarray(_t5_bucket(rel))
    return pl.pallas_call(
        _bias_kernel,
        in_specs=[pl.BlockSpec(memory_space=pltpu.SMEM),
                  pl.BlockSpec(bucket.shape, lambda: (0, 0))],
        out_specs=pl.BlockSpec((3, N_KV_HEADS, 2 * BLOCK, 2 * SPAN), lambda: (0, 0, 0, 0)),
        out_shape=jax.ShapeDtypeStruct((3, N_KV_HEADS, 2 * BLOCK, 2 * SPAN), F32),
        name="t5bias",
    )(rel_table.astype(F32), bucket)


def _dup_heads(w):
    parts = []
    for h in range(N_KV_HEADS):
        wh = w[..., h * HEAD_DIM:(h + 1) * HEAD_DIM]
        parts += [wh, wh]
    return jnp.concatenate(parts, axis=-1)


def _mean_matrix(width):
    blk = np.arange(width) // HEAD_DIM
    return jnp.asarray((blk[:, None] == blk[None, :]).astype(np.float32) / HEAD_DIM, dtype=BF16)


def kernel(x, norm_mix_g, w_in, b_gate, vnorm_g, w_spatial, b_spatial, q_norm_g, k_norm_g,
           attn_sink, rel_bias_table, w_proj_a, w_proj_b, w_out, norm_ffn_g, w_router,
           w_gate_e, w_up_e, w_down_e):
    B, S, _ = x.shape
    T = B * S
    cap = CAPACITY_FACTOR * S // N_EXPERTS
    assert S % TQ_MIX == 0 and T % TM_INPROJ == 0 and B % MOE_GROUP == 0
    assert B % min(TOPK_BATCHES, B) == 0
    xf = x.reshape(T, D_MODEL)
    l = 0

    w = w_in[l]
    o_k = 2 * A_WIDTH + B_WIDTH
    o_v = o_k + KV_WIDTH
    o_g = o_v + KV_WIDTH
    w_all = w.astype(BF16)
    w_kv = jnp.concatenate([_dup_heads(w[:, o_k:o_v]), _dup_heads(w[:, o_v:o_g])], axis=1).astype(BF16)
    qg = jnp.tile(q_norm_g[l], N_HEADS)[None, :] * (HEAD_DIM ** -0.5)
    kg = jnp.tile(k_norm_g[l], 2 * N_KV_HEADS)[None, :]
    wsp = w_spatial[l].reshape(A_GROUPS // 2, 2, CHUNK, CHUNK).transpose(0, 2, 1, 3)
    wsp = wsp.reshape(A_GROUPS // 2, CHUNK, 2 * CHUNK).astype(BF16)
    bsp = jnp.repeat(b_spatial[l].T, A_GROUP_DIM, axis=1)
    bias = _attention_bias(rel_bias_table)
    sink = attn_sink[l].astype(F32).reshape(N_KV_HEADS, 2, 2)
    sink = jnp.broadcast_to(sink.transpose(0, 2, 1)[:, :, :, None, None],
                            (N_KV_HEADS, 2, 2, BLOCK, 1)).reshape(N_KV_HEADS, 2, 2 * BLOCK, 1)
    wr = w_router[l]
    wrh = wr.astype(BF16)
    wrl = (wr - wrh.astype(F32)).astype(BF16)
    pad = jnp.zeros((D_MODEL, LANES - N_EXPERTS), BF16)
    wr2 = jnp.concatenate([wrh, pad, wrl, pad], axis=1)
    tri = jnp.asarray(np.triu(np.ones((LANES, LANES), np.float32), k=1), dtype=BF16)

    (u, vn, q, k, v, ga, gb), (wg, wu, wd) = _inproj(
        xf, norm_mix_g[l][None, :], w_all, w_kv, b_gate[l][None, :], vnorm_g[l][None, :], qg, kg,
        _mean_matrix(B_WIDTH), _mean_matrix(KV_DUP), (w_gate_e[l], w_up_e[l], w_down_e[l]))
    x1, aff = _mix(B, S, u, vn, q, k, v, ga, gb, xf, wsp, bsp, bias, sink,
                   w_proj_a[l].astype(BF16), w_proj_b[l].astype(BF16), w_out[l].astype(BF16),
                   norm_ffn_g[l][None, :], wr2)
    afft = aff.reshape(B, S, N_EXPERTS).transpose(0, 2, 1).reshape(B * N_EXPERTS, S)
    idx, gates = _topk(B, S, cap, afft, tri)
    out = _moe(B, S, cap, idx.reshape(-1), x1, gates.reshape(B, N_EXPERTS, cap, 1), wg, wu, wd,
               norm_ffn_g[l][None, :])
    return out.reshape(B, S, D_MODEL)
```

```python
import functools

import numpy as np
import jax
import jax.numpy as jnp
from jax import lax
from jax.experimental import pallas as pl
from jax.experimental.pallas import tpu as pltpu

F32 = jnp.float32
BF16 = jnp.bfloat16

D_MODEL = 1024
A_GROUPS = 8
A_GROUP_DIM = 64
A_WIDTH = A_GROUPS * A_GROUP_DIM
CHUNK = 128
N_HEADS = 8
N_KV_HEADS = 2
HEAD_DIM = 64
Q_GROUP = N_HEADS // N_KV_HEADS
B_WIDTH = N_HEADS * HEAD_DIM
KV_WIDTH = N_KV_HEADS * HEAD_DIM
WINDOW = 128
BLOCK = 128
SPAN = BLOCK + 2 * WINDOW
N_BUCKETS = 32
MAX_DISTANCE = 128
N_EXPERTS = 16
EXPERT_FF = 2048
CAPACITY_FACTOR = 2
EPS = 1e-6

LANES = 128
KV_DUP = 2 * KV_WIDTH
TM_INPROJ = 512
TQ_MIX = 1024
STAGE_BLOCKS = 2
FF_CHUNK = 256
MOE_GROUP = 2
SCATTER_ROWS = 8
TOPK_BATCHES = 8
TOPK_ROWS = 8
VMEM_LIMIT = 56 * 1024 * 1024
INF_BITS = 0x7F800000


def _rms(x, g):
    return x * lax.rsqrt(jnp.mean(x * x, axis=-1, keepdims=True) + EPS) * g


def _inproj_kernel(x_ref, g_ref, w_ref, wkv_ref, bg_ref, vg_ref, qg_ref, kg_ref, bdq_ref, bdk_ref,
                   ewg_ref, ewu_ref, ewd_ref,
                   u_ref, vn_ref, q_ref, k_ref, v_ref, ga_ref, gb_ref, bwg_ref, bwu_ref, bwd_ref):
    bwg_ref[...] = ewg_ref[...].astype(BF16)
    bwu_ref[...] = ewu_ref[...].astype(BF16)
    bwd_ref[...] = ewd_ref[...].astype(BF16)

    hb = _rms(x_ref[...], g_ref[...]).astype(BF16)

    def seg(lo, hi):
        return jnp.dot(hb, w_ref[:, lo:hi], preferred_element_type=F32)

    o_va, o_q, o_k = A_WIDTH, 2 * A_WIDTH, 2 * A_WIDTH + B_WIDTH
    o_ga = o_k + 2 * KV_WIDTH
    o_gb = o_ga + D_MODEL
    ga_ref[...] = jax.nn.sigmoid(seg(o_ga, o_gb) + bg_ref[:, :D_MODEL]).astype(BF16)
    gb_ref[...] = jax.nn.sigmoid(seg(o_gb, o_gb + D_MODEL) + bg_ref[:, D_MODEL:]).astype(BF16)
    zq = seg(o_q, o_k)
    msq = jnp.dot((zq * zq).astype(BF16), bdq_ref[...], preferred_element_type=F32)
    q_ref[...] = (zq * lax.rsqrt(msq + EPS) * qg_ref[...]).astype(BF16)
    u_ref[...] = jax.nn.gelu(seg(0, o_va)).astype(BF16)
    gv = jax.nn.gelu(seg(o_va, o_q))
    vn_ref[...] = _rms(gv, vg_ref[...]).astype(BF16)
    zkv = jnp.dot(hb, wkv_ref[...], preferred_element_type=F32)
    zk = zkv[:, :KV_DUP]
    msk = jnp.dot((zk * zk).astype(BF16), bdk_ref[...], preferred_element_type=F32)
    k_ref[...] = (zk * lax.rsqrt(msk + EPS) * kg_ref[...]).astype(BF16)
    v_ref[...] = zkv[:, KV_DUP:].astype(BF16)


def _inproj(xf, norm_g, w_all, w_kv, b_gate, vnorm_g, qg, kg, bdq, bdk, expert_w):
    T = xf.shape[0]
    tm = TM_INPROJ
    steps = T // tm
    row = lambda w: pl.BlockSpec((tm, w), lambda i: (i, 0))
    full = lambda a: pl.BlockSpec(a.shape, lambda i: (0,) * a.ndim)
    out_w = (A_WIDTH, A_WIDTH, B_WIDTH, KV_DUP, KV_DUP, D_MODEL, D_MODEL)
    flat = [w.reshape(-1, w.shape[-1]) for w in expert_w]
    assert all(w.shape[0] % (steps * 16) == 0 for w in flat)
    slab = lambda w: pl.BlockSpec((w.shape[0] // steps, w.shape[1]), lambda i: (i, 0))
    outs = pl.pallas_call(
        _inproj_kernel,
        grid=(steps,),
        in_specs=[row(D_MODEL), full(norm_g), full(w_all), full(w_kv),
                  full(b_gate), full(vnorm_g), full(qg), full(kg), full(bdq), full(bdk)]
                 + [slab(w) for w in flat],
        out_specs=[row(w) for w in out_w] + [slab(w) for w in flat],
        out_shape=[jax.ShapeDtypeStruct((T, w), BF16) for w in out_w]
                  + [jax.ShapeDtypeStruct(w.shape, BF16) for w in flat],
        compiler_params=pltpu.CompilerParams(dimension_semantics=("arbitrary",),
                                             vmem_limit_bytes=VMEM_LIMIT),
        name="inproj",
    )(xf, norm_g, w_all, w_kv, b_gate, vnorm_g, qg, kg, bdq, bdk, *flat)
    return outs[:len(out_w)], [o.reshape(w.shape) for o, w in zip(outs[len(out_w):], expert_w)]


def _block_diag_pair(slab, lo_mask):
    zero = jnp.zeros_like(slab)
    return jnp.concatenate([jnp.where(lo_mask, slab, zero), jnp.where(lo_mask, zero, slab)], axis=0)


def _mix_kernel(u_ref, vn_ref, q_ref, kc_ref, kp_ref, kn_ref, vc_ref, vp_ref, vx_ref,
                ga_ref, gb_ref, x_ref, wsp_ref, bsp_ref, bias_ref, sink_ref,
                wpa_ref, wpb_ref, wo_ref, ng_ref, wr_ref,
                x1_ref, aff_ref, kbuf, vbuf, abuf, obuf):
    tq = x_ref.shape[0]
    nblk = tq // BLOCK
    i = pl.program_id(1)
    last = pl.num_programs(1) - 1
    lo_mask = lax.broadcasted_iota(jnp.int32, (1, LANES), 1) < HEAD_DIM

    kbuf[0:WINDOW, :] = kp_ref[...]
    kbuf[WINDOW:WINDOW + tq, :] = kc_ref[...]
    kbuf[WINDOW + tq:, :] = kn_ref[...]
    vbuf[0:WINDOW, :] = vp_ref[...]
    vbuf[WINDOW:WINDOW + tq, :] = vc_ref[...]
    vbuf[WINDOW + tq:, :] = vx_ref[...]

    def spatial_gating(c):
        rows = slice(c * CHUNK, (c + 1) * CHUNK)
        parts = []
        for j in range(A_GROUPS // 2):
            bd = _block_diag_pair(vn_ref[rows, j * LANES:(j + 1) * LANES], lo_mask)
            parts.append(jnp.dot(wsp_ref[j], bd, preferred_element_type=F32))
        mixed = jnp.concatenate(parts, axis=1) + bsp_ref[...]
        abuf[rows, :] = (u_ref[rows, :].astype(F32) * mixed).astype(BF16)

    vrow = lax.broadcasted_iota(jnp.int32, (2 * SPAN, LANES), 0)
    vcol = lax.broadcasted_iota(jnp.int32, (2 * SPAN, LANES), 1)
    ones_cols = jnp.where((vrow < SPAN) == (vcol < HEAD_DIM), 1.0, 0.0).astype(BF16)

    def attention_stages(blocks):
        units = [(n, kh) for n in blocks for kh in range(N_KV_HEADS)]
        st = {}

        def geometry(n, kh):
            if n == 0:
                var = jnp.where(i == 0, 0, 1)
            elif n == nblk - 1:
                var = jnp.where(i == last, 2, 1)
            else:
                var = 1
            return (slice(n * BLOCK, (n + 1) * BLOCK), slice(n * BLOCK, n * BLOCK + SPAN),
                    slice(kh * LANES, (kh + 1) * LANES), var)

        def scores():
            for n, kh in units:
                rows, span, ksl, var = geometry(n, kh)
                kk = _block_diag_pair(kbuf[span, ksl], lo_mask)
                q2 = jnp.concatenate([q_ref[rows, (2 * kh) * LANES:(2 * kh + 1) * LANES],
                                      q_ref[rows, (2 * kh + 1) * LANES:(2 * kh + 2) * LANES]], axis=0)
                st["s", n, kh] = (lax.dot_general(q2, kk, (((1,), (1,)), ((), ())),
                                                  preferred_element_type=F32)
                                  + bias_ref[var, kh])

        def numerators():
            for n, kh in units:
                p2, m2 = [], []
                for half in range(2):
                    sh = st["s", n, kh][:, half * SPAN:(half + 1) * SPAN]
                    m = jnp.maximum(jnp.max(sh, axis=-1, keepdims=True), sink_ref[kh, half])
                    p2.append(jnp.exp(sh - m).astype(BF16))
                    m2.append(m)
                st["p", n, kh] = jnp.concatenate(p2, axis=1)
                st["m", n, kh] = m2

        def values():
            for n, kh in units:
                rows, span, ksl, _ = geometry(n, kh)
                vv = jnp.concatenate([_block_diag_pair(vbuf[span, ksl], lo_mask), ones_cols], axis=1)
                o2 = jnp.dot(st["p", n, kh], vv, preferred_element_type=F32)
                m2 = st["m", n, kh]
                sink_term = jnp.where(lo_mask, jnp.exp(sink_ref[kh, 0] - m2[0]),
                                      jnp.exp(sink_ref[kh, 1] - m2[1]))
                o = o2[:, :LANES] * (1.0 / (o2[:, LANES:] + sink_term))
                obuf[rows, (2 * kh) * LANES:(2 * kh + 1) * LANES] = o[:BLOCK].astype(BF16)
                obuf[rows, (2 * kh + 1) * LANES:(2 * kh + 2) * LANES] = o[BLOCK:].astype(BF16)

        return [scores, numerators, values]

    def projection_pieces(rows):
        st = {}

        def branches():
            pa = jnp.dot(abuf[rows, :], wpa_ref[...], preferred_element_type=F32)
            pb = jnp.dot(obuf[rows, :], wpb_ref[...], preferred_element_type=F32)
            st["merged"] = (ga_ref[rows, :].astype(F32) * pa + gb_ref[rows, :].astype(F32) * pb).astype(BF16)

        def output():
            st["x1"] = x_ref[rows, :] + jnp.dot(st["merged"], wo_ref[...], preferred_element_type=F32)
            x1_ref[rows, :] = st["x1"]

        def router():
            h2 = _rms(st["x1"], ng_ref[...])
            hi = h2.astype(BF16)
            lo = (h2 - hi.astype(F32)).astype(BF16)
            both = jnp.dot(hi, wr_ref[...], preferred_element_type=F32)
            logits = (both[:, :LANES] + jnp.dot(lo, wr_ref[:, :LANES], preferred_element_type=F32)
                      + both[:, LANES:])[:, :N_EXPERTS]
            ex = jnp.exp(logits - jnp.max(logits, axis=-1, keepdims=True))
            aff_ref[rows, :] = ex / jnp.sum(ex, axis=-1, keepdims=True)

        return [branches, output, router]

    group = STAGE_BLOCKS
    ngroups = nblk // group
    grows = group * BLOCK
    proj = [projection_pieces(slice(j * grows, (j + 1) * grows)) for j in range(ngroups)]
    gating = [functools.partial(spatial_gating, c) for c in range(nblk)]
    for _ in range(group):
        gating.pop(0)()
    routers = []
    for j in range(ngroups):
        fill = list(proj[j - 1][:-1]) if j else []
        while len(fill) < 3 and gating:
            fill.append(gating.pop(0))
        for stage in attention_stages(range(j * group, (j + 1) * group)):
            stage()
            if fill:
                fill.pop(0)()
        for f in fill:
            f()
        if j:
            routers.append(proj[j - 1][-1])
    for f in gating:
        f()
    for piece in proj[-1][:-1]:
        piece()
    for router in routers + [proj[-1][-1]]:
        router()


def _mix(B, S, u, vn, q, k, v, ga, gb, xf, wsp, bsp, bias, sink, wpa, wpb, wo, ng, wr2):
    T = B * S
    tq = TQ_MIX
    nq = S // tq
    bpq = tq // BLOCK
    nb = S // BLOCK
    row = lambda w: pl.BlockSpec((tq, w), lambda b, i: (b * nq + i, 0))
    prev = pl.BlockSpec((BLOCK, KV_DUP), lambda b, i: (b * nb + jnp.maximum(i * bpq - 1, 0), 0))
    nxt = pl.BlockSpec((BLOCK, KV_DUP), lambda b, i: (b * nb + jnp.minimum(i * bpq + bpq, nb - 1), 0))
    full = lambda a: pl.BlockSpec(a.shape, lambda b, i: (0,) * a.ndim, pipeline_mode=pl.Buffered(1))
    return pl.pallas_call(
        _mix_kernel,
        grid=(B, nq),
        in_specs=[row(A_WIDTH), row(A_WIDTH), row(B_WIDTH),
                  row(KV_DUP), prev, nxt, row(KV_DUP), prev, nxt,
                  row(D_MODEL), row(D_MODEL), row(D_MODEL),
                  full(wsp), full(bsp), full(bias), full(sink),
                  full(wpa), full(wpb), full(wo), full(ng), full(wr2)],
        out_specs=[row(D_MODEL), row(N_EXPERTS)],
        out_shape=[jax.ShapeDtypeStruct((T, D_MODEL), F32),
                   jax.ShapeDtypeStruct((T, N_EXPERTS), F32)],
        scratch_shapes=[pltpu.VMEM((tq + 2 * WINDOW, KV_DUP), BF16),
                        pltpu.VMEM((tq + 2 * WINDOW, KV_DUP), BF16),
                        pltpu.VMEM((tq, A_WIDTH), BF16),
                        pltpu.VMEM((tq, B_WIDTH), BF16)],
        compiler_params=pltpu.CompilerParams(dimension_semantics=("arbitrary", "arbitrary"),
                                             vmem_limit_bytes=VMEM_LIMIT),
        name="mix",
    )(u, vn, q, k, k, k, v, v, v, ga, gb, xf, wsp, bsp, bias, sink, wpa, wpb, wo, ng, wr2)


def _topk_kernel(afft_ref, tri_ref, idx_ref, gate_ref, cum, cend, *, cap):
    R, S = afft_ref.shape
    nblk = S // LANES
    j = pl.program_id(1)
    lane = lax.broadcasted_iota(jnp.int32, (1, LANES), 1)

    @pl.when(j == 0)
    def _():
        bits = pltpu.bitcast(afft_ref[...], jnp.int32)

        def search(_, carry):
            lo, hi = carry
            mid = lo + ((hi - lo) >> 1)
            cnt = jnp.sum(jnp.where(bits >= mid, 1.0, 0.0), axis=1, keepdims=True)
            ge = cnt >= cap
            return jnp.where(ge, mid, lo), jnp.where(ge, hi, mid)

        lo0 = jnp.zeros((R, 1), jnp.int32)
        hi0 = jnp.full((R, 1), INF_BITS, jnp.int32)
        thr, _ = lax.fori_loop(0, 31, search, (lo0, hi0))

        def prefix(mask):
            outs, carry, ends = [], jnp.zeros((R, 1), F32), jnp.zeros((R, LANES), F32)
            for jj in range(nblk):
                mj = mask[:, jj * LANES:(jj + 1) * LANES]
                outs.append(jnp.dot(mj.astype(BF16), tri_ref[...], preferred_element_type=F32) + carry)
                carry = carry + jnp.sum(mj, axis=1, keepdims=True)
                ends = jnp.where(lane == jj, carry, ends)
            return jnp.concatenate(outs, axis=1), ends

        gt = jnp.where(bits > thr, 1.0, 0.0)
        eq = jnp.where(bits == thr, 1.0, 0.0)
        need = cap - jnp.sum(gt, axis=1, keepdims=True)
        sel = gt + eq * jnp.where(prefix(eq)[0] < need, 1.0, 0.0)
        before, ends = prefix(sel)
        cum[...] = before + sel
        cend[...] = ends

    slot = lax.broadcasted_iota(jnp.int32, (cap, 1), 0).astype(F32)

    def by_block(row):
        return jnp.concatenate([row[:, b * LANES:(b + 1) * LANES] for b in range(nblk)], axis=0)

    def compact(rr):
        r = j * TOPK_ROWS + rr
        done = jnp.where((cend[pl.ds(r, 1), :] <= slot) & (lane < nblk), 1.0, 0.0)
        blk = jnp.sum(done, axis=1, keepdims=True).astype(jnp.int32)
        pick = jnp.where(lane == blk, 1.0, 0.0).astype(BF16)[:, :nblk]
        cum_b = jnp.dot(pick, by_block(cum[pl.ds(r, 1), :]).astype(BF16), preferred_element_type=F32)
        off = jnp.sum(jnp.where(cum_b <= slot, 1.0, 0.0), axis=1, keepdims=True).astype(jnp.int32)
        idx_ref[rr] = blk * LANES + off
        a = by_block(afft_ref[pl.ds(r, 1), :])
        a_hi = a.astype(BF16)
        r1 = a - a_hi.astype(F32)
        a_mid = r1.astype(BF16)
        a_lo = (r1 - a_mid.astype(F32)).astype(BF16)
        aff_b = (jnp.dot(pick, a_hi, preferred_element_type=F32)
                 + jnp.dot(pick, a_mid, preferred_element_type=F32)
                 + jnp.dot(pick, a_lo, preferred_element_type=F32))
        gate_ref[rr] = jnp.sum(jnp.where(lane == off, aff_b, 0.0), axis=1, keepdims=True)

    for rr in range(TOPK_ROWS):
        compact(rr)


def _topk(B, S, cap, afft, tri):
    nb = min(TOPK_BATCHES, B)
    R = nb * N_EXPERTS
    steps = R // TOPK_ROWS
    out_block = pl.BlockSpec((TOPK_ROWS, cap, 1), lambda i, j: (i * steps + j, 0, 0))
    return pl.pallas_call(
        functools.partial(_topk_kernel, cap=cap),
        grid=(B // nb, steps),
        in_specs=[pl.BlockSpec((R, S), lambda i, j: (i, 0)),
                  pl.BlockSpec(tri.shape, lambda i, j: (0, 0))],
        out_specs=[out_block, out_block],
        out_shape=[jax.ShapeDtypeStruct((B * N_EXPERTS, cap, 1), jnp.int32),
                   jax.ShapeDtypeStruct((B * N_EXPERTS, cap, 1), F32)],
        scratch_shapes=[pltpu.VMEM((R, S), F32), pltpu.VMEM((R, LANES), F32)],
        compiler_params=pltpu.CompilerParams(dimension_semantics=("arbitrary", "arbitrary"),
                                             vmem_limit_bytes=VMEM_LIMIT),
        name="topk",
    )(afft, tri)


def _moe_kernel(idx_ref, x1_hbm, gate_ref, wg_ref, wu_ref, wd_ref, ng_ref, out_hbm,
                acc, hbuf, xg, ybuf, sem_in, sem, *, S, cap):
    g = pl.program_id(0)
    e2 = pl.program_id(1)
    k = pl.program_id(2)
    nexp = 2 * pl.num_programs(1)
    half = D_MODEL // 2
    rows0 = g * (MOE_GROUP * S)
    hi_mask = jnp.int32(-65536)

    def lists(seq, ee):
        return ((g * MOE_GROUP + seq) * nexp + ee) * cap

    def gather_jobs(seq, ee, par):
        base = lists(seq, ee)

        def job(c0):
            for c in range(c0, c0 + SCATTER_ROWS):
                r = seq * S + idx_ref[base + c]
                xg[par, seq * cap + c:seq * cap + c + 1, :] = hbuf[pl.ds(r, 1), :]
        return [functools.partial(job, c0) for c0 in range(0, cap, SCATTER_ROWS)]

    def scatter_jobs(seq, ee, par):
        base = lists(seq, ee)

        def job(c0):
            rows = [seq * S + idx_ref[base + c] for c in range(c0, c0 + SCATTER_ROWS)]
            old = [acc[pl.ds(r, 1), :] for r in rows]
            for j, r in enumerate(rows):
                c = seq * cap + c0 + j
                acc[pl.ds(r, 1), :] = old[j] + ybuf[par, c:c + 1, :]
        return [functools.partial(job, c0) for c0 in range(0, cap, SCATTER_ROWS)]

    def ffn(par, accumulate, jobs):
        n_up, n_down = wg_ref.shape[2] // FF_CHUNK, D_MODEL // FF_CHUNK
        unit = -(-len(jobs) // (2 * n_up + n_down - 1))
        jobs = list(jobs)

        def run_share(units):
            for job in jobs[:units * unit]:
                job()
            del jobs[:units * unit]

        p = xg[par]
        lo = pltpu.bitcast(p << 16, F32).astype(BF16)
        hi = pltpu.bitcast(p & hi_mask, F32).astype(BF16)
        x = jnp.concatenate([lo, hi], axis=1)
        hid = []
        for f in range(n_up):
            cols = slice(f * FF_CHUNK, (f + 1) * FF_CHUNK)
            gt = jnp.dot(x, wg_ref[0, :, cols], preferred_element_type=F32)
            run_share(1)
            up = jnp.dot(x, wu_ref[0, :, cols], preferred_element_type=F32)
            hid.append((gt * jax.nn.sigmoid(gt) * up).astype(BF16))
            run_share(1)
        hid = jnp.concatenate(hid, axis=1)
        gate = jnp.concatenate([gate_ref[s, 0] for s in range(MOE_GROUP)], axis=0)
        for n in range(n_down):
            cols = slice(n * FF_CHUNK, (n + 1) * FF_CHUNK)
            y = jnp.dot(hid, wd_ref[0, :, cols], preferred_element_type=F32) * gate
            ybuf[par, :, cols] = ybuf[par, :, cols] + y if accumulate else y
            run_share(1 if n < n_down - 1 else 0)
        assert not jobs

    @pl.when((e2 == 0) & (k == 0))
    def _():
        def load(i):
            start = i * cap if isinstance(i, int) else pl.multiple_of(i * cap, cap)
            return pltpu.make_async_copy(x1_hbm.at[pl.ds(rows0 + start, cap), :],
                                         acc.at[pl.ds(start, cap), :], sem_in.at[i])

        for i in range(sem_in.shape[0]):
            load(i).start()

        def pack(i, _):
            load(i).wait()
            rows = pl.ds(pl.multiple_of(i * cap, cap), cap)
            hb = _rms(acc[rows, :], ng_ref[...]).astype(BF16).astype(F32)
            u = pltpu.bitcast(hb, jnp.int32)
            hbuf[rows, :] = lax.shift_right_logical(u[:, :half], 16) | (u[:, half:] & hi_mask)
            return 0
        lax.fori_loop(0, MOE_GROUP * S // cap, pack, 0)
        ybuf[1] = jnp.zeros(ybuf.shape[1:], F32)
        for seq in range(MOE_GROUP):
            for job in gather_jobs(seq, 0, 0):
                job()

    for kk in range(4):
        @pl.when(k == kk)
        def _(par=kk // 2, fh=kk % 2):
            e = 2 * e2 + par
            sc = scatter_jobs(fh, jnp.maximum(e - 1, 0), 1 - par)
            ga = gather_jobs(fh, jnp.minimum(e + 1, nexp - 1), 1 - par)
            ffn(par, fh == 1, [job for pair in zip(sc, ga) for job in pair])

    @pl.when((e2 == pl.num_programs(1) - 1) & (k == 3))
    def _():
        for seq in range(MOE_GROUP):
            for job in scatter_jobs(seq, nexp - 1, 1):
                job()
        cp = pltpu.make_async_copy(acc, out_hbm.at[pl.ds(rows0, MOE_GROUP * S), :], sem)
        cp.start()
        cp.wait()


def _moe(B, S, cap, idx_flat, x1, gates, wg, wu, wd, ng):
    G = MOE_GROUP
    T = B * S
    fhw = EXPERT_FF // 2
    ex = lambda e2, k: 2 * e2 + k // 2
    fh = lambda k: k % 2
    grid_spec = pltpu.PrefetchScalarGridSpec(
        num_scalar_prefetch=1,
        grid=(B // G, N_EXPERTS // 2, 4),
        in_specs=[pl.BlockSpec(memory_space=pl.ANY),
                  pl.BlockSpec((G, 1, cap, 1), lambda g, e2, k, idx: (g, ex(e2, k), 0, 0)),
                  pl.BlockSpec((1, D_MODEL, fhw), lambda g, e2, k, idx: (ex(e2, k), 0, fh(k))),
                  pl.BlockSpec((1, D_MODEL, fhw), lambda g, e2, k, idx: (ex(e2, k), 0, fh(k))),
                  pl.BlockSpec((1, fhw, D_MODEL), lambda g, e2, k, idx: (ex(e2, k), fh(k), 0)),
                  pl.BlockSpec((1, D_MODEL), lambda g, e2, k, idx: (0, 0))],
        out_specs=pl.BlockSpec(memory_space=pl.ANY),
        scratch_shapes=[pltpu.VMEM((G * S, D_MODEL), F32),
                        pltpu.VMEM((G * S, D_MODEL // 2), jnp.int32),
                        pltpu.VMEM((2, G * cap, D_MODEL // 2), jnp.int32),
                        pltpu.VMEM((2, G * cap, D_MODEL), F32),
                        pltpu.SemaphoreType.DMA((G * S // cap,)),
                        pltpu.SemaphoreType.DMA(())],
    )
    return pl.pallas_call(
        functools.partial(_moe_kernel, S=S, cap=cap),
        grid_spec=grid_spec,
        out_shape=jax.ShapeDtypeStruct((T, D_MODEL), F32),
        input_output_aliases={1: 0},
        compiler_params=pltpu.CompilerParams(dimension_semantics=("arbitrary",) * 3,
                                             vmem_limit_bytes=VMEM_LIMIT),
        name="moe",
    )(idx_flat, x1, gates, wg, wu, wd, ng)


def _t5_bucket(rel):
    nb = N_BUCKETS // 2
    max_exact = nb // 2
    ret = (rel > 0).astype(np.int32) * nb
    n = np.abs(rel)
    large = max_exact + (np.log(np.maximum(n, 1) / max_exact) / np.log(MAX_DISTANCE / max_exact)
                         * (nb - max_exact)).astype(np.int32)
    large = np.minimum(large, nb - 1)
    return (ret + np.where(n < max_exact, n, large)).astype(np.int32)


def _bias_kernel(table_ref, bucket_ref, out_ref):
    bucket = bucket_ref[...]
    row = lax.broadcasted_iota(jnp.int32, (BLOCK, SPAN), 0)
    col = lax.broadcasted_iota(jnp.int32, (BLOCK, SPAN), 1)
    band = jnp.abs(col - WINDOW - row) <= WINDOW
    valid = (band & (col >= WINDOW), band, band & (col < WINDOW + BLOCK))
    for h in range(N_HEADS):
        b = jnp.zeros((BLOCK, SPAN), F32)
        for n in range(N_BUCKETS):
            b = jnp.where(bucket == n, table_ref[n, h], b)
        kh, slab, half = h // Q_GROUP, (h % Q_GROUP) // 2, h % 2
        for var in range(3):
            out_ref[var, kh, slab * BLOCK:(slab + 1) * BLOCK, half * SPAN:(half + 1) * SPAN] = (
                jnp.where(valid[var], b, -jnp.inf))


def _attention_bias(rel_table):
    rel = (np.arange(SPAN)[None, :] - WINDOW) - np.arange(BLOCK)[:, None]
    bucket = jnp.asarray(_t5_bucket(rel))
    return pl.pallas_call(
        _bias_kernel,
        in_specs=[pl.BlockSpec(memory_space=pltpu.SMEM),
                  pl.BlockSpec(bucket.shape, lambda: (0, 0))],
        out_specs=pl.BlockSpec((3, N_KV_HEADS, 2 * BLOCK, 2 * SPAN), lambda: (0, 0, 0, 0)),
        out_shape=jax.ShapeDtypeStruct((3, N_KV_HEADS, 2 * BLOCK, 2 * SPAN), F32),
        name="t5bias",
    )(rel_table.astype(F32), bucket)


def _dup_heads(w):
    parts = []
    for h in range(N_KV_HEADS):
        wh = w[..., h * HEAD_DIM:(h + 1) * HEAD_DIM]
        parts += [wh, wh]
    return jnp.concatenate(parts, axis=-1)


def _mean_matrix(width):
    blk = np.arange(width) // HEAD_DIM
    return jnp.asarray((blk[:, None] == blk[None, :]).astype(np.float32) / HEAD_DIM, dtype=BF16)


def kernel(x, norm_mix_g, w_in, b_gate, vnorm_g, w_spatial, b_spatial, q_norm_g, k_norm_g,
           attn_sink, rel_bias_table, w_proj_a, w_proj_b, w_out, norm_ffn_g, w_router,
           w_gate_e, w_up_e, w_down_e):
    B, S, _ = x.shape
    T = B * S
    cap = CAPACITY_FACTOR * S // N_EXPERTS
    assert S % TQ_MIX == 0 and T % TM_INPROJ == 0 and B % MOE_GROUP == 0
    assert B % min(TOPK_BATCHES, B) == 0
    xf = x.reshape(T, D_MODEL)
    l = 0

    w = w_in[l]
    o_k = 2 * A_WIDTH + B_WIDTH
    o_v = o_k + KV_WIDTH
    o_g = o_v + KV_WIDTH
    w_all = w.astype(BF16)
    w_kv = jnp.concatenate([_dup_heads(w[:, o_k:o_v]), _dup_heads(w[:, o_v:o_g])], axis=1).astype(BF16)
    qg = jnp.tile(q_norm_g[l], N_HEADS)[None, :] * (HEAD_DIM ** -0.5)
    kg = jnp.tile(k_norm_g[l], 2 * N_KV_HEADS)[None, :]
    wsp = w_spatial[l].reshape(A_GROUPS // 2, 2, CHUNK, CHUNK).transpose(0, 2, 1, 3)
    wsp = wsp.reshape(A_GROUPS // 2, CHUNK, 2 * CHUNK).astype(BF16)
    bsp = jnp.repeat(b_spatial[l].T, A_GROUP_DIM, axis=1)
    bias = _attention_bias(rel_bias_table)
    sink = attn_sink[l].astype(F32).reshape(N_KV_HEADS, 2, 2)
    sink = jnp.broadcast_to(sink.transpose(0, 2, 1)[:, :, :, None, None],
                            (N_KV_HEADS, 2, 2, BLOCK, 1)).reshape(N_KV_HEADS, 2, 2 * BLOCK, 1)
    wr = w_router[l]
    wrh = wr.astype(BF16)
    wrl = (wr - wrh.astype(F32)).astype(BF16)
    pad = jnp.zeros((D_MODEL, LANES - N_EXPERTS), BF16)
    wr2 = jnp.concatenate([wrh, pad, wrl, pad], axis=1)
    tri = jnp.asarray(np.triu(np.ones((LANES, LANES), np.float32), k=1), dtype=BF16)

    (u, vn, q, k, v, ga, gb), (wg, wu, wd) = _inproj(
        xf, norm_mix_g[l][None, :], w_all, w_kv, b_gate[l][None, :], vnorm_g[l][None, :], qg, kg,
        _mean_matrix(B_WIDTH), _mean_matrix(KV_DUP), (w_gate_e[l], w_up_e[l], w_down_e[l]))
    x1, aff = _mix(B, S, u, vn, q, k, v, ga, gb, xf, wsp, bsp, bias, sink,
                   w_proj_a[l].astype(BF16), w_proj_b[l].astype(BF16), w_out[l].astype(BF16),
                   norm_ffn_g[l][None, :], wr2)
    afft = aff.reshape(B, S, N_EXPERTS).transpose(0, 2, 1).reshape(B * N_EXPERTS, S)
    idx, gates = _topk(B, S, cap, afft, tri)
    out = _moe(B, S, cap, idx.reshape(-1), x1, gates.reshape(B, N_EXPERTS, cap, 1), wg, wu, wd,
               norm_ffn_g[l][None, :])
    return out.reshape(B, S, D_MODEL)
```
